```python
import math
import jax, jax.numpy as jnp
from jax import lax
import numpy as np

D_MODEL = 1024
BATCH = 8
SEQ = 2048
DEPTH = 1
DEC_BATCH = 128
DEC_SEQ = 1
PAST_LEN = 16384
PAGE_SIZE = 128

D_RNN = 1024
RNN_BLOCKS = 8
RNN_BW = D_RNN // RNN_BLOCKS
CONV_W = 4
RG_C = 8.0
GLA_HEADS = 4
GLA_DK = D_MODEL // 2 // GLA_HEADS
GLA_DV = D_MODEL // GLA_HEADS
GLA_RANK = 16
GLA_TAU = 16.0
GLA_CHUNK = 64
D_FF = 2816
EPS = 1e-6

IN_SIZES = (D_RNN, D_RNN, GLA_HEADS * GLA_DK, GLA_HEADS * GLA_DK, GLA_HEADS * GLA_DV,
            GLA_HEADS * GLA_DV, GLA_RANK, D_MODEL, D_MODEL)
D_IN = sum(IN_SIZES)

kernel_name = "hawk_gla_parallel_macaron_step"


def rmsnorm(x, g):
    xf = x.astype(jnp.float32)
    y = xf * lax.rsqrt(jnp.mean(xf * xf, axis=-1, keepdims=True) + EPS)
    return (y * g.astype(jnp.float32)).astype(x.dtype)


def swiglu(x, w_gu, w_down):
    gate, up = jnp.split(x @ w_gu, 2, axis=-1)
    return (jax.nn.silu(gate) * up) @ w_down


def causal_depthwise_conv(xr, buf, w, b):
    T = xr.shape[1]
    xp = jnp.concatenate([buf.astype(xr.dtype), xr], axis=1)
    y = b
    for j in range(CONV_W):
        y = y + xp[:, j:j + T] * w[j]
    return y, xp[:, T:]


def linear_scan(a, b, h0):
    def combine(c1, c2):
        a1, b1 = c1
        a2, b2 = c2
        return a1 * a2, a2 * b1 + b2
    a_cum, h = lax.associative_scan(combine, (a, b), axis=1)
    return h + a_cum * h0[:, None]


def rglru(xc, h0, w_a, b_a, w_x, b_x, lam, reset_first):
    xf = xc.astype(jnp.float32)
    B, T, _ = xf.shape
    xb = xf.reshape(B, T, RNN_BLOCKS, RNN_BW)
    r = jax.nn.sigmoid(jnp.einsum('btni,nij->btnj', xb, w_a.astype(jnp.float32)).reshape(B, T, D_RNN) + b_a)
    i = jax.nn.sigmoid(jnp.einsum('btni,nij->btnj', xb, w_x.astype(jnp.float32)).reshape(B, T, D_RNN) + b_x)
    log_a = -RG_C * r * jax.nn.softplus(-lam.astype(jnp.float32))
    a = jnp.exp(log_a)
    mult = jnp.sqrt(jnp.maximum(-jnp.expm1(2.0 * log_a), 0.0))
    if reset_first:
        mult = mult.at[:, 0].set(1.0)
    h = linear_scan(a, mult * i * xf, h0.astype(jnp.float32))
    return h, h[:, -1]


def gla_chunked(q, k, v, g, S0):
    B, T, H, DK = q.shape
    DV = v.shape[-1]
    C = min(GLA_CHUNK, T)
    pad = (-T) % C
    if pad:
        pw = ((0, 0), (0, pad), (0, 0), (0, 0))
        q, k, v, g = (jnp.pad(t, pw) for t in (q, k, v, g))
    N = (T + pad) // C
    q, k, g = (t.reshape(B, N, C, H, DK) for t in (q, k, g))
    v = v.reshape(B, N, C, H, DV)
    bcum = jnp.cumsum(g, axis=2)
    q_i = q * jnp.exp(bcum)
    k_i = k * jnp.exp(-bcum)
    mask = jnp.tril(jnp.ones((C, C), dtype=bool))
    att = jnp.where(mask, jnp.einsum('bnchk,bnshk->bnhcs', q_i, k_i), 0.0)
    o_intra = jnp.einsum('bnhcs,bnshv->bnchv', att, v)
    b_last = bcum[:, :, -1]
    k_end = k * jnp.exp(b_last[:, :, None] - bcum)

    def step(S, inp):
        kk, vv, dl = inp
        S_new = S * jnp.exp(dl)[..., None] + jnp.einsum('bchk,bchv->bhkv', kk, vv)
        return S_new, S

    xs = (jnp.moveaxis(k_end, 1, 0), jnp.moveaxis(v, 1, 0), jnp.moveaxis(b_last, 1, 0))
    S_final, S_starts = lax.scan(step, S0, xs)
    S_starts = jnp.moveaxis(S_starts, 0, 1)
    o_inter = jnp.einsum('bnchk,bnhkv->bnchv', q_i, S_starts)
    o = (o_intra + o_inter).reshape(B, N * C, H, DV)[:, :T]
    return o, S_final


def token_mixing(u, h0, conv0, S0, reset_first, w_in, conv_w, conv_b, rg_w_a, rg_b_a, rg_w_x, rg_b_x,
                 rg_lambda, gla_w_lr, gla_b_lr, gla_norm_g, w_branch_rnn, w_branch_gla, w_out):
    Bsz, T, _ = u.shape
    f32 = jnp.float32
    split_at = np.cumsum(IN_SIZES)[:-1].tolist()
    xr, yr, q, k, v, og, lr, gate_rnn, gate_gla = jnp.split(u @ w_in, split_at, axis=-1)
    xc, conv_new = causal_depthwise_conv(xr, conv0, conv_w, conv_b)
    h, h_last = rglru(xc, h0, rg_w_a, rg_b_a, rg_w_x, rg_b_x, rg_lambda, reset_first)
    o_rnn = (h * jax.nn.gelu(yr.astype(f32))).astype(u.dtype)
    qh = q.reshape(Bsz, T, GLA_HEADS, GLA_DK).astype(f32) * (GLA_DK ** -0.5)
    kh = k.reshape(Bsz, T, GLA_HEADS, GLA_DK).astype(f32)
    vh = v.reshape(Bsz, T, GLA_HEADS, GLA_DV).astype(f32)
    log_alpha = jax.nn.log_sigmoid((lr @ gla_w_lr + gla_b_lr).astype(f32)) / GLA_TAU
    o, S_new = gla_chunked(qh, kh, vh, log_alpha.reshape(Bsz, T, GLA_HEADS, GLA_DK), S0.astype(f32))
    o = o * lax.rsqrt(jnp.mean(o * o, axis=-1, keepdims=True) + EPS) * gla_norm_g.astype(f32)
    o = o * jax.nn.silu(og.reshape(Bsz, T, GLA_HEADS, GLA_DV).astype(f32))
    o_gla = o.reshape(Bsz, T, GLA_HEADS * GLA_DV).astype(u.dtype)
    merged = jax.nn.sigmoid(gate_rnn) * (o_rnn @ w_branch_rnn) + jax.nn.sigmoid(gate_gla) * (o_gla @ w_branch_gla)
    return merged @ w_out, h_last.astype(u.dtype), conv_new.astype(u.dtype), S_new.astype(u.dtype)


def decoder_layer(x, h0, conv0, S0, reset_first, norm_g, ffn1_w_gu, ffn1_w_down, w_in, conv_w, conv_b,
                  rg_w_a, rg_b_a, rg_w_x, rg_b_x, rg_lambda, gla_w_lr, gla_b_lr, gla_norm_g,
                  w_branch_rnn, w_branch_gla, w_out, ffn2_w_gu, ffn2_w_down):
    x = x + 0.5 * rmsnorm(swiglu(rmsnorm(x, norm_g[0]), ffn1_w_gu, ffn1_w_down), norm_g[1])
    mix, h_last, conv_new, S_new = token_mixing(
        rmsnorm(x, norm_g[2]), h0, conv0, S0, reset_first, w_in, conv_w, conv_b, rg_w_a, rg_b_a,
        rg_w_x, rg_b_x, rg_lambda, gla_w_lr, gla_b_lr, gla_norm_g, w_branch_rnn, w_branch_gla, w_out)
    x = x + rmsnorm(mix, norm_g[3])
    x = x + 0.5 * rmsnorm(swiglu(rmsnorm(x, norm_g[4]), ffn2_w_gu, ffn2_w_down), norm_g[5])
    return x, h_last, conv_new, S_new


def setup_inputs(seed: int = 0) -> dict:
    key = jax.random.key(seed)
    ks = iter(jax.random.split(key, 32))
    f32 = jnp.float32

    def nrm(shape, scale):
        return jax.random.normal(next(ks), shape, f32) * scale

    u = jax.random.uniform(next(ks), (DEPTH, D_RNN), f32, 0.9, 0.999)
    p = u ** (1.0 / RG_C)
    rg_lambda = jnp.log(p) - jnp.log1p(-p)
    return {
        "x_prompt": nrm((BATCH, SEQ, D_MODEL), 1.0),
        "x_sample": nrm((DEC_BATCH, DEC_SEQ, D_MODEL), 1.0),
        "state_rnn_h": nrm((DEPTH, DEC_BATCH, D_RNN), 0.5),
        "state_rnn_conv": nrm((DEPTH, DEC_BATCH, CONV_W - 1, D_RNN), 1.0),
        "state_gla": nrm((DEPTH, DEC_BATCH, GLA_HEADS, GLA_DK, GLA_DV), 1.0),
        "norm_gains": 1.0 + nrm((DEPTH, 6, D_MODEL), 0.02),
        "ffn1_w_gu": nrm((DEPTH, D_MODEL, 2 * D_FF), D_MODEL ** -0.5),
        "ffn1_w_down": nrm((DEPTH, D_FF, D_MODEL), D_FF ** -0.5),
        "w_in": nrm((DEPTH, D_MODEL, D_IN), D_MODEL ** -0.5),
        "conv_w": nrm((DEPTH, CONV_W, D_RNN), CONV_W ** -0.5),
        "conv_b": nrm((DEPTH, D_RNN), 0.01),
        "rg_w_a": nrm((DEPTH, RNN_BLOCKS, RNN_BW, RNN_BW), RNN_BW ** -0.5),
        "rg_b_a": nrm((DEPTH, D_RNN), 0.01),
        "rg_w_x": nrm((DEPTH, RNN_BLOCKS, RNN_BW, RNN_BW), RNN_BW ** -0.5),
        "rg_b_x": nrm((DEPTH, D_RNN), 0.01),
        "rg_lambda": rg_lambda,
        "gla_w_lr": nrm((DEPTH, GLA_RANK, GLA_HEADS * GLA_DK), GLA_RANK ** -0.5),
        "gla_b_lr": nrm((DEPTH, GLA_HEADS * GLA_DK), 0.01),
        "gla_norm_g": 1.0 + nrm((DEPTH, GLA_DV), 0.02),
        "w_branch_rnn": nrm((DEPTH, D_RNN, D_MODEL), D_RNN ** -0.5),
        "w_branch_gla": nrm((DEPTH, GLA_HEADS * GLA_DV, D_MODEL), (GLA_HEADS * GLA_DV) ** -0.5),
        "w_out": nrm((DEPTH, D_MODEL, D_MODEL), D_MODEL ** -0.5),
        "ffn2_w_gu": nrm((DEPTH, D_MODEL, 2 * D_FF), D_MODEL ** -0.5),
        "ffn2_w_down": nrm((DEPTH, D_FF, D_MODEL), D_FF ** -0.5),
    }


def reference(x_prompt, x_sample, state_rnn_h, state_rnn_conv, state_gla, norm_gains, ffn1_w_gu,
              ffn1_w_down, w_in, conv_w, conv_b, rg_w_a, rg_b_a, rg_w_x, rg_b_x, rg_lambda, gla_w_lr,
              gla_b_lr, gla_norm_g, w_branch_rnn, w_branch_gla, w_out, ffn2_w_gu, ffn2_w_down):
    dt = x_prompt.dtype
    Bp = x_prompt.shape[0]
    yp, ys = x_prompt, x_sample
    hp_l, cp_l, sp_l, hs_l, cs_l, ss_l = [], [], [], [], [], []
    for l in range(DEPTH):
        lw = (norm_gains[l], ffn1_w_gu[l], ffn1_w_down[l], w_in[l], conv_w[l], conv_b[l], rg_w_a[l],
              rg_b_a[l], rg_w_x[l], rg_b_x[l], rg_lambda[l], gla_w_lr[l], gla_b_lr[l], gla_norm_g[l],
              w_branch_rnn[l], w_branch_gla[l], w_out[l], ffn2_w_gu[l], ffn2_w_down[l])
        h0 = jnp.zeros((Bp, D_RNN), dt)
        c0 = jnp.zeros((Bp, CONV_W - 1, D_RNN), dt)
        s0 = jnp.zeros((Bp, GLA_HEADS, GLA_DK, GLA_DV), dt)
        yp, hp, cp, sp = decoder_layer(yp, h0, c0, s0, True, *lw)
        ys, hs, cs, ss = decoder_layer(ys, state_rnn_h[l], state_rnn_conv[l], state_gla[l], False, *lw)
        hp_l.append(hp); cp_l.append(cp); sp_l.append(sp)
        hs_l.append(hs); cs_l.append(cs); ss_l.append(ss)
    return (yp, ys, jnp.stack(hp_l), jnp.stack(cp_l), jnp.stack(sp_l),
            jnp.stack(hs_l), jnp.stack(cs_l), jnp.stack(ss_l))
```

```python
import functools
import math

import jax
import jax.numpy as jnp
from jax import lax
from jax.experimental import pallas as pl
from jax.experimental.pallas import tpu as pltpu

D_MODEL = 1024
D_RNN = 1024
RNN_BLOCKS = 8
RNN_BW = D_RNN // RNN_BLOCKS
CONV_W = 4
RG_C = 8.0
GLA_HEADS = 4
GLA_DK = 128
GLA_DV = 256
GLA_RANK = 16
GLA_TAU = 16.0
GLA_CHUNK = 64
D_FF = 2816
EPS = 1e-6

D_QK = GLA_HEADS * GLA_DK
D_V = GLA_HEADS * GLA_DV
OFF_Q = 2 * D_RNN
OFF_K = OFF_Q + D_QK
OFF_V = OFF_K + D_QK
OFF_OG = OFF_V + D_V
OFF_LR = OFF_OG + D_V
OFF_GATES = OFF_LR + GLA_RANK
D_IN = OFF_GATES + 2 * D_MODEL

LANES = 128
SUBLANES = 8
VMEM_LIMIT = 56 * 1024 * 1024

R_NORM = 0
R_CONVW = 6
R_CONVB = 10
R_BA = 11
R_BX = 12
R_LAM = 13
R_BLR = 14
R_GNORM = 15
N_VEC_ROWS = 16

bf16 = jnp.bfloat16
f32 = jnp.float32


def _dot(a, b):
    return jnp.dot(a, b, preferred_element_type=f32)


def _rms(x, g):
    return x * lax.rsqrt(jnp.mean(x * x, axis=-1, keepdims=True) + EPS) * g


def _softplus(x):
    return jnp.maximum(x, 0.0) + jnp.log1p(jnp.exp(-jnp.abs(x)))


def _gelu_tanh(x):
    c = math.sqrt(2.0 / math.pi)
    return 0.5 * x * (1.0 + jnp.tanh(c * (x + 0.044715 * (x * x * x))))


def _ffn(x, g_pre, g_post, wgu_ref, wd_ref):
    u = _rms(x, g_pre).astype(bf16)
    gu = _dot(u, wgu_ref[...])
    h = (jax.nn.silu(gu[:, :D_FF]) * gu[:, D_FF:]).astype(bf16)
    y = _dot(h, wd_ref[...])
    return x + 0.5 * _rms(y, g_post)


def _rglru_gates(xc, vec_ref, rgbd_ref):
    xcb = xc.astype(bf16)
    slab = 2 * RNN_BW
    ra, ri = [], []
    for s in range(D_RNN // slab):
        gi = _dot(xcb[:, s * slab:(s + 1) * slab], rgbd_ref[s])
        ra.append(gi[:, :slab])
        ri.append(gi[:, slab:])
    r = jax.nn.sigmoid(jnp.concatenate(ra, axis=1) + vec_ref[R_BA:R_BA + 1, :])
    i = jax.nn.sigmoid(jnp.concatenate(ri, axis=1) + vec_ref[R_BX:R_BX + 1, :])
    log_a = -RG_C * r * _softplus(-vec_ref[R_LAM:R_LAM + 1, :])
    a = jnp.exp(log_a)
    mult = jnp.sqrt(jnp.maximum(-(jnp.tanh(log_a) * (a * a + 1.0)), 0.0))
    return a, mult, i


def _head_norm_gate(o, og, vec_ref):
    outs = []
    for h in range(GLA_HEADS):
        oh = o[:, h * GLA_DV:(h + 1) * GLA_DV]
        gh = vec_ref[R_GNORM:R_GNORM + 1, h * GLA_DV:(h + 1) * GLA_DV]
        outs.append(oh * lax.rsqrt(jnp.mean(oh * oh, axis=-1, keepdims=True) + EPS) * gh)
    return jnp.concatenate(outs, axis=1) * jax.nn.silu(og)


def _merge_out(x, o_rnn_b, o_gla_b, gates, vec_ref, wbr_ref, wbg_ref, wo_ref):
    ya = _dot(o_rnn_b, wbr_ref[...])
    yb = _dot(o_gla_b, wbg_ref[...])
    merged = jax.nn.sigmoid(gates[:, :D_MODEL]) * ya + jax.nn.sigmoid(gates[:, D_MODEL:]) * yb
    mix = _dot(merged.astype(bf16), wo_ref[...])
    return x + _rms(mix, vec_ref[R_NORM + 3:R_NORM + 4, :])


def _ffn_kernel(x_ref, vec_ref, wgu_ref, wd_ref, o_ref, *, row_pre):
    o_ref[...] = _ffn(x_ref[...], vec_ref[row_pre:row_pre + 1, :], vec_ref[row_pre + 1:row_pre + 2, :],
                      wgu_ref, wd_ref)


def _const_spec(shape):
    nd = len(shape)
    return pl.BlockSpec(shape, lambda *_: (0,) * nd, pipeline_mode=pl.Buffered(1))


def _ffn_call(x2d, vecs, wgu, wd, *, row_pre, tm, name):
    n = x2d.shape[0]
    return pl.pallas_call(
        functools.partial(_ffn_kernel, row_pre=row_pre),
        grid=(n // tm,),
        in_specs=[pl.BlockSpec((tm, D_MODEL), lambda i: (i, 0)),
                  _const_spec(vecs.shape), _const_spec(wgu.shape), _const_spec(wd.shape)],
        out_specs=pl.BlockSpec((tm, D_MODEL), lambda i: (i, 0)),
        out_shape=jax.ShapeDtypeStruct((n, D_MODEL), f32),
        compiler_params=pltpu.CompilerParams(dimension_semantics=("arbitrary",), vmem_limit_bytes=VMEM_LIMIT),
        name=name,
    )(x2d, vecs, wgu, wd)


def _mix_prompt_kernel(x_ref, vec_ref, wmain_ref, wlr_ref, wlr2_ref, wg_ref, rgbd_ref, wbr_ref, wbg_ref, wo_ref,
                       y_ref, h_ref, conv_ref, s_ref,
                       xr_ext, hc, S, hbuf, ogla, *, tc):
    t = pl.program_id(1)
    nt = pl.num_programs(1)

    @pl.when(t == 0)
    def _():
        xr_ext[0:SUBLANES, :] = jnp.zeros((SUBLANES, D_RNN), f32)
        hc[...] = jnp.zeros_like(hc)
        S[...] = jnp.zeros_like(S)

    x = x_ref[...]
    u = _rms(x, vec_ref[R_NORM + 2:R_NORM + 3, :]).astype(bf16)

    xy = _dot(u, wmain_ref[:, 0:2 * D_RNN])
    xr = xy[:, :D_RNN]
    yr = xy[:, D_RNN:]
    xr_ext[SUBLANES:SUBLANES + tc, :] = xr
    xc = vec_ref[R_CONVB:R_CONVB + 1, :]
    for j in range(CONV_W):
        off = SUBLANES - (CONV_W - 1) + j
        xc = xc + xr_ext[off:off + tc, :] * vec_ref[R_CONVW + j:R_CONVW + j + 1, :]
    xr_ext[0:SUBLANES, :] = xr_ext[tc:tc + SUBLANES, :]

    a, mult, gi = _rglru_gates(xc, vec_ref, rgbd_ref)
    row = lax.broadcasted_iota(jnp.int32, (tc, D_RNN), 0)
    mult = jnp.where(jnp.logical_and(row == 0, t == 0), 1.0, mult)
    b = mult * gi * xc

    rin = jnp.bitwise_and(row, SUBLANES - 1)
    s = 1
    while s < SUBLANES:
        a_sh = pltpu.roll(a, s, 0)
        b_sh = pltpu.roll(b, s, 0)
        m = rin >= s
        b = jnp.where(m, a * b_sh + b, b)
        a = jnp.where(m, a * a_sh, a)
        s *= 2
    hprev = hc[...]
    for g in range(tc // SUBLANES):
        sl = slice(g * SUBLANES, (g + 1) * SUBLANES)
        hg = a[sl, :] * hprev + b[sl, :]
        hbuf[sl, :] = hg
        hprev = jnp.broadcast_to(hg[SUBLANES - 1:SUBLANES, :], (SUBLANES, D_RNN))
    hc[...] = hprev
    o_rnn = (hbuf[...] * _gelu_tanh(yr)).astype(bf16)

    qkvg = _dot(u, wmain_ref[:, OFF_Q:OFF_LR])
    lr = _dot(u, wlr_ref[...])
    z = _dot(lr.astype(bf16), wlr2_ref[...]) + vec_ref[R_BLR:R_BLR + 1, 0:D_QK]
    glog = -_softplus(-z) * (1.0 / GLA_TAU)

    c = GLA_CHUNK
    ri = lax.broadcasted_iota(jnp.int32, (c, c), 0)
    ci = lax.broadcasted_iota(jnp.int32, (c, c), 1)
    tril = ri >= ci
    ltri = tril.astype(bf16)
    scale = GLA_DK ** -0.5
    for n in range(tc // c):
        rs = slice(n * c, (n + 1) * c)
        gc = glog[rs, :]
        g1 = gc.astype(bf16)
        r1 = gc - g1.astype(f32)
        g2 = r1.astype(bf16)
        g3 = (r1 - g2.astype(f32)).astype(bf16)
        bcum = _dot(ltri, g1) + _dot(ltri, g2) + _dot(ltri, g3)
        blast = bcum[c - 1:c, :]
        qi = (qkvg[rs, 0:D_QK] * scale * jnp.exp(bcum)).astype(bf16)
        kc = qkvg[rs, D_QK:2 * D_QK]
        ki = (kc * jnp.exp(-bcum)).astype(bf16)
        kend = (kc * jnp.exp(blast - bcum)).astype(bf16)
        vb = qkvg[rs, 2 * D_QK:2 * D_QK + D_V].astype(bf16)
        dec = jnp.exp(blast)
        for h in range(GLA_HEADS):
            ks = slice(h * GLA_DK, (h + 1) * GLA_DK)
            vs = slice(h * GLA_DV, (h + 1) * GLA_DV)
            att = lax.dot_general(qi[:, ks], ki[:, ks], (((1,), (1,)), ((), ())), preferred_element_type=f32)
            att = jnp.where(tril, att, 0.0).astype(bf16)
            sh = S[h]
            o = _dot(att, vb[:, vs]) + _dot(qi[:, ks], sh.astype(bf16))
            ogla[rs, vs] = o
            dcol = jnp.transpose(jnp.broadcast_to(dec[:, ks], (GLA_DK, GLA_DK)))
            dcol = jnp.concatenate([dcol, dcol], axis=1)
            upd = lax.dot_general(kend[:, ks], vb[:, vs], (((0,), (0,)), ((), ())), preferred_element_type=f32)
            S[h] = sh * dcol + upd

    o_gla = _head_norm_gate(ogla[...], qkvg[:, 2 * D_QK + D_V:], vec_ref).astype(bf16)

    gates = _dot(u, wg_ref[...])
    y_ref[...] = _merge_out(x, o_rnn, o_gla, gates, vec_ref, wbr_ref, wbg_ref, wo_ref)

    @pl.when(t == nt - 1)
    def _():
        h_ref[...] = hc[0:1, :]
        conv_ref[...] = xr_ext[SUBLANES - (CONV_W - 1):SUBLANES, :]
        s_ref[...] = S[...]


def _mix_prompt_call(x, vecs, wmain, wlr, wlr2, wg, rgbd, wbr, wbg, wo, *, tc):
    bsz, seq, _ = x.shape
    consts = (vecs, wmain, wlr, wlr2, wg, rgbd, wbr, wbg, wo)
    return pl.pallas_call(
        functools.partial(_mix_prompt_kernel, tc=tc),
        grid=(bsz, seq // tc),
        in_specs=[pl.BlockSpec((None, tc, D_MODEL), lambda b, t: (b, t, 0))] + [_const_spec(w.shape) for w in consts],
        out_specs=[pl.BlockSpec((None, tc, D_MODEL), lambda b, t: (b, t, 0)),
                   pl.BlockSpec((None, 1, D_RNN), lambda b, t: (b, 0, 0)),
                   pl.BlockSpec((None, CONV_W - 1, D_RNN), lambda b, t: (b, 0, 0)),
                   pl.BlockSpec((None, GLA_HEADS, GLA_DK, GLA_DV), lambda b, t: (b, 0, 0, 0))],
        out_shape=[jax.ShapeDtypeStruct((bsz, seq, D_MODEL), f32),
                   jax.ShapeDtypeStruct((bsz, 1, D_RNN), f32),
                   jax.ShapeDtypeStruct((bsz, CONV_W - 1, D_RNN), f32),
                   jax.ShapeDtypeStruct((bsz, GLA_HEADS, GLA_DK, GLA_DV), f32)],
        scratch_shapes=[pltpu.VMEM((tc + SUBLANES, D_RNN), f32),
                        pltpu.VMEM((SUBLANES, D_RNN), f32),
                        pltpu.VMEM((GLA_HEADS, GLA_DK, GLA_DV), f32),
                        pltpu.VMEM((tc, D_RNN), f32),
                        pltpu.VMEM((tc, D_V), f32)],
        compiler_params=pltpu.CompilerParams(dimension_semantics=("arbitrary", "arbitrary"),
                                             vmem_limit_bytes=VMEM_LIMIT),
        name="mix_prompt",
    )(x, *consts)


def _sample_pre_kernel(x_ref, h0_ref, c0_ref, vec_ref, wgu_ref, wd_ref, wmain_ref, wlr_ref, wlr2_ref, wg_ref, rgbd_ref,
                       x1_ref, hn_ref, cn_ref, ornn_ref, qkvg_ref, glog_ref, gates_ref):
    x1 = _ffn(x_ref[...], vec_ref[R_NORM:R_NORM + 1, :], vec_ref[R_NORM + 1:R_NORM + 2, :], wgu_ref, wd_ref)
    x1_ref[...] = x1
    u = _rms(x1, vec_ref[R_NORM + 2:R_NORM + 3, :]).astype(bf16)
    xy = _dot(u, wmain_ref[:, 0:2 * D_RNN])
    xr = xy[:, :D_RNN]
    yr = xy[:, D_RNN:]
    xc = vec_ref[R_CONVB:R_CONVB + 1, :]
    for j in range(CONV_W - 1):
        xc = xc + c0_ref[:, j * D_RNN:(j + 1) * D_RNN] * vec_ref[R_CONVW + j:R_CONVW + j + 1, :]
    xc = xc + xr * vec_ref[R_CONVW + CONV_W - 1:R_CONVW + CONV_W, :]
    cn_ref[:, 0:(CONV_W - 2) * D_RNN] = c0_ref[:, D_RNN:(CONV_W - 1) * D_RNN]
    cn_ref[:, (CONV_W - 2) * D_RNN:] = xr
    a, mult, gi = _rglru_gates(xc, vec_ref, rgbd_ref)
    h = a * h0_ref[...] + mult * gi * xc
    hn_ref[...] = h
    ornn_ref[...] = (h * _gelu_tanh(yr)).astype(bf16)
    qkvg_ref[...] = _dot(u, wmain_ref[:, OFF_Q:OFF_LR])
    lr = _dot(u, wlr_ref[...])
    z = _dot(lr.astype(bf16), wlr2_ref[...]) + vec_ref[R_BLR:R_BLR + 1, 0:D_QK]
    glog_ref[...] = -_softplus(-z) * (1.0 / GLA_TAU)
    gates_ref[...] = _dot(u, wg_ref[...])


def _sample_pre_call(xs, h0, c0, vecs, wgu, wd, wmain, wlr, wlr2, wg, rgbd):
    n = xs.shape[0]
    args = (xs, h0, c0, vecs, wgu, wd, wmain, wlr, wlr2, wg, rgbd)
    outs = [((n, D_MODEL), f32), ((n, D_RNN), f32), ((n, (CONV_W - 1) * D_RNN), f32), ((n, D_RNN), bf16),
            ((n, OFF_LR - OFF_Q), f32), ((n, D_QK), f32), ((n, 2 * D_MODEL), f32)]
    return pl.pallas_call(
        _sample_pre_kernel,
        grid=(1,),
        in_specs=[_const_spec(a.shape) for a in args],
        out_specs=[pl.BlockSpec(s, lambda i: (0, 0)) for s, _ in outs],
        out_shape=[jax.ShapeDtypeStruct(s, d) for s, d in outs],
        compiler_params=pltpu.CompilerParams(dimension_semantics=("arbitrary",), vmem_limit_bytes=VMEM_LIMIT),
        name="sample_pre",
    )(*args)


def _col_bcast(rows8, width):
    tiled = jnp.concatenate([rows8] * (LANES // SUBLANES), axis=0)
    tt = jnp.transpose(tiled)
    return [jnp.broadcast_to(tt[:, j:j + 1], (LANES, width)) for j in range(SUBLANES)]


def _sample_gla_kernel(q_ref, k_ref, v_ref, g_ref, s0_ref, sn_ref, o_ref):
    scale = GLA_DK ** -0.5
    for h in range(GLA_HEADS):
        ks = slice(h * GLA_DK, (h + 1) * GLA_DK)
        vs = slice(h * GLA_DV, (h + 1) * GLA_DV)
        dcols = _col_bcast(jnp.exp(g_ref[:, ks]), GLA_DV)
        kcols = _col_bcast(k_ref[:, ks], GLA_DV)
        qcols = _col_bcast(q_ref[:, ks] * scale, GLA_DV)
        for j in range(SUBLANES):
            vrow = v_ref[j:j + 1, vs]
            sn = dcols[j] * s0_ref[j, h] + kcols[j] * vrow
            sn_ref[j, h] = sn
            o_ref[j:j + 1, vs] = jnp.sum(qcols[j] * sn, axis=0, keepdims=True)


def _sample_gla_call(q, k, v, g, s0):
    n = q.shape[0]
    sb = (SUBLANES, GLA_HEADS, GLA_DK, GLA_DV)
    return pl.pallas_call(
        _sample_gla_kernel,
        grid=(n // SUBLANES,),
        in_specs=[pl.BlockSpec((SUBLANES, D_QK), lambda i: (i, 0)),
                  pl.BlockSpec((SUBLANES, D_QK), lambda i: (i, 0)),
                  pl.BlockSpec((SUBLANES, D_V), lambda i: (i, 0)),
                  pl.BlockSpec((SUBLANES, D_QK), lambda i: (i, 0)),
                  pl.BlockSpec(sb, lambda i: (i, 0, 0, 0))],
        out_specs=[pl.BlockSpec(sb, lambda i: (i, 0, 0, 0)),
                   pl.BlockSpec((SUBLANES, D_V), lambda i: (i, 0))],
        out_shape=[jax.ShapeDtypeStruct(s0.shape, f32), jax.ShapeDtypeStruct((n, D_V), f32)],
        compiler_params=pltpu.CompilerParams(dimension_semantics=("arbitrary",), vmem_limit_bytes=VMEM_LIMIT),
        name="sample_gla",
    )(q, k, v, g, s0)


def _sample_post_kernel(x1_ref, ornn_ref, o_ref, og_ref, gates_ref, vec_ref, wbr_ref, wbg_ref, wo_ref, wgu_ref, wd_ref,
                        y_ref):
    o_gla = _head_norm_gate(o_ref[...], og_ref[...], vec_ref).astype(bf16)
    x2 = _merge_out(x1_ref[...], ornn_ref[...], o_gla, gates_ref[...], vec_ref, wbr_ref, wbg_ref, wo_ref)
    y_ref[...] = _ffn(x2, vec_ref[R_NORM + 4:R_NORM + 5, :], vec_ref[R_NORM + 5:R_NORM + 6, :], wgu_ref, wd_ref)


def _sample_post_call(x1, ornn, o, og, gates, vecs, wbr, wbg, wo, wgu, wd):
    n = x1.shape[0]
    args = (x1, ornn, o, og, gates, vecs, wbr, wbg, wo, wgu, wd)
    return pl.pallas_call(
        _sample_post_kernel,
        grid=(1,),
        in_specs=[_const_spec(a.shape) for a in args],
        out_specs=pl.BlockSpec((n, D_MODEL), lambda i: (0, 0)),
        out_shape=jax.ShapeDtypeStruct((n, D_MODEL), f32),
        compiler_params=pltpu.CompilerParams(dimension_semantics=("arbitrary",), vmem_limit_bytes=VMEM_LIMIT),
        name="sample_post",
    )(*args)


def _block_diag_gates(w_a, w_x):
    def bd(w):
        w = w.reshape(RNN_BLOCKS // 2, 2, RNN_BW, RNN_BW)
        z = jnp.zeros_like(w[:, 0])
        top = jnp.concatenate([w[:, 0], z], axis=2)
        bot = jnp.concatenate([z, w[:, 1]], axis=2)
        return jnp.concatenate([top, bot], axis=1)
    return jnp.concatenate([bd(w_a), bd(w_x)], axis=2).astype(bf16)


def kernel(x_prompt, x_sample, state_rnn_h, state_rnn_conv, state_gla, norm_gains, ffn1_w_gu, ffn1_w_down, w_in, conv_w, conv_b, rg_w_a, rg_b_a, rg_w_x, rg_b_x, rg_lambda, gla_w_lr, gla_b_lr, gla_norm_g, w_branch_rnn, w_branch_gla, w_out, ffn2_w_gu, ffn2_w_down):
    assert w_in.shape == (1, D_MODEL, D_IN) and x_sample.shape[1] == 1
    bsz, seq, _ = x_prompt.shape
    nsmp = x_sample.shape[0]

    vecs = jnp.concatenate([
        norm_gains[0], conv_w[0], conv_b, rg_b_a, rg_b_x, rg_lambda,
        jnp.pad(gla_b_lr, ((0, 0), (0, D_MODEL - D_QK))), jnp.tile(gla_norm_g, (1, GLA_HEADS))], axis=0).astype(f32)
    w1gu = ffn1_w_gu[0].astype(bf16)
    w1d = ffn1_w_down[0].astype(bf16)
    w2gu = ffn2_w_gu[0].astype(bf16)
    w2d = ffn2_w_down[0].astype(bf16)
    wmain = w_in[0, :, :OFF_LR].astype(bf16)
    wlr = jnp.pad(w_in[0, :, OFF_LR:OFF_GATES], ((0, 0), (0, LANES - GLA_RANK))).astype(bf16)
    wlr2 = jnp.pad(gla_w_lr[0], ((0, LANES - GLA_RANK), (0, 0))).astype(bf16)
    wg = w_in[0, :, OFF_GATES:].astype(bf16)
    rgbd = _block_diag_gates(rg_w_a[0], rg_w_x[0])
    wbr = w_branch_rnn[0].astype(bf16)
    wbg = w_branch_gla[0].astype(bf16)
    wo = w_out[0].astype(bf16)

    xp = x_prompt.reshape(bsz * seq, D_MODEL)
    x1 = _ffn_call(xp, vecs, w1gu, w1d, row_pre=R_NORM, tm=512, name="ffn1_prompt")
    x2, hp, cp, sp = _mix_prompt_call(x1.reshape(bsz, seq, D_MODEL), vecs, wmain, wlr, wlr2, wg, rgbd, wbr, wbg, wo,
                                      tc=256)
    yp = _ffn_call(x2.reshape(bsz * seq, D_MODEL), vecs, w2gu, w2d, row_pre=R_NORM + 4, tm=512, name="ffn2_prompt")

    xs = x_sample.reshape(nsmp, D_MODEL)
    c0 = state_rnn_conv[0].reshape(nsmp, (CONV_W - 1) * D_RNN)
    x1s, hs, cs, ornn, qkvg, glog, gates = _sample_pre_call(
        xs, state_rnn_h[0], c0, vecs, w1gu, w1d, wmain, wlr, wlr2, wg, rgbd)
    ss, osmp = _sample_gla_call(qkvg[:, 0:D_QK], qkvg[:, D_QK:2 * D_QK], qkvg[:, 2 * D_QK:2 * D_QK + D_V], glog,
                                state_gla[0])
    ys = _sample_post_call(x1s, ornn, osmp, qkvg[:, 2 * D_QK + D_V:], gates, vecs, wbr, wbg, wo, w2gu, w2d)

    return (yp.reshape(bsz, seq, D_MODEL), ys.reshape(nsmp, 1, D_MODEL),
            hp.reshape(1, bsz, D_RNN), cp[None], sp[None],
            hs[None], cs.reshape(1, nsmp, CONV_W - 1, D_RNN), ss[None])
```

```python
import functools
import math

import jax
import jax.numpy as jnp
from jax import lax
from jax.experimental import pallas as pl
from jax.experimental.pallas import tpu as pltpu

D_MODEL = 1024
D_RNN = 1024
RNN_BLOCKS = 8
RNN_BW = D_RNN // RNN_BLOCKS
CONV_W = 4
RG_C = 8.0
GLA_HEADS = 4
GLA_DK = 128
GLA_DV = 256
GLA_RANK = 16
GLA_TAU = 16.0
GLA_CHUNK = 64
D_FF = 2816
EPS = 1e-6

D_QK = GLA_HEADS * GLA_DK
D_V = GLA_HEADS * GLA_DV
OFF_Q = 2 * D_RNN
OFF_K = OFF_Q + D_QK
OFF_V = OFF_K + D_QK
OFF_OG = OFF_V + D_V
OFF_LR = OFF_OG + D_V
OFF_GATES = OFF_LR + GLA_RANK
D_IN = OFF_GATES + 2 * D_MODEL

LANES = 128
SUBLANES = 8
VMEM_LIMIT = 56 * 1024 * 1024
GLA_BLOCK = 256

R_NORM = 0
R_CONVW = 6
R_CONVB = 10
R_BA = 11
R_BX = 12
R_LAM = 13
R_BLR = 14
R_GNORM = 15

bf16 = jnp.bfloat16
f32 = jnp.float32


def _w(packed):
    return pltpu.bitcast(packed, bf16)


def _dot(a, b):
    return jnp.dot(a, b, preferred_element_type=f32)


def _rms(x, g):
    return x * lax.rsqrt(jnp.mean(x * x, axis=-1, keepdims=True) + EPS) * g


def _sigmoid(x):
    return 0.5 * jnp.tanh(0.5 * x) + 0.5


def _silu(x):
    hx = 0.5 * x
    return hx * jnp.tanh(hx) + hx


def _softplus(x):
    return jnp.maximum(x, 0.0) + jnp.log1p(jnp.exp(-jnp.abs(x)))


def _gelu_tanh(x):
    c = math.sqrt(2.0 / math.pi)
    return 0.5 * x * (1.0 + jnp.tanh(c * (x + 0.044715 * (x * x * x))))


def _ffn(x, g_pre, g_post, wgu_ref, wd_ref):
    u = _rms(x, g_pre).astype(bf16)
    gu = _dot(u, _w(wgu_ref[...]))
    h = (_silu(gu[:, :D_FF]) * gu[:, D_FF:]).astype(bf16)
    y = _dot(h, _w(wd_ref[...]))
    return x + 0.5 * _rms(y, g_post)


def _rglru_preact(xc, rgbd_ref):
    xcb = xc.astype(bf16)
    slab = 2 * RNN_BW
    ra, ri = [], []
    for s in range(D_RNN // slab):
        gi = _dot(xcb[:, s * slab:(s + 1) * slab], _w(rgbd_ref[s]))
        ra.append(gi[:, :slab])
        ri.append(gi[:, slab:])
    return jnp.concatenate(ra, axis=1), jnp.concatenate(ri, axis=1)


def _rglru_coeffs(ra, ri, vec_ref):
    r = _sigmoid(ra + vec_ref[R_BA:R_BA + 1, :])
    i = _sigmoid(ri + vec_ref[R_BX:R_BX + 1, :])
    log_a = -RG_C * r * _softplus(-vec_ref[R_LAM:R_LAM + 1, :])
    a = jnp.exp(log_a)
    mult = jnp.sqrt(jnp.maximum(-(jnp.tanh(log_a) * (a * a + 1.0)), 0.0))
    return a, mult, i


def _head_norm(oh, vec_ref, h):
    gh = vec_ref[R_GNORM:R_GNORM + 1, h * GLA_DV:(h + 1) * GLA_DV]
    return oh * lax.rsqrt(jnp.mean(oh * oh, axis=-1, keepdims=True) + EPS) * gh


def _decay_log(u, wlr_ref, wlr2_ref, vec_ref):
    lr = _dot(u, _w(wlr_ref[...]))
    z = _dot(lr.astype(bf16), _w(wlr2_ref[...])) + vec_ref[R_BLR:R_BLR + 1, 0:D_QK]
    return -_softplus(-z) * (1.0 / GLA_TAU)


def _ffn_kernel(x_ref, vec_ref, wgu_ref, wd_ref, o_ref, *, row_pre):
    o_ref[...] = _ffn(x_ref[...], vec_ref[row_pre:row_pre + 1, :], vec_ref[row_pre + 1:row_pre + 2, :],
                      wgu_ref, wd_ref)


def _const_spec(shape):
    nd = len(shape)
    return pl.BlockSpec(shape, lambda *_: (0,) * nd, pipeline_mode=pl.Buffered(1))


def _ffn_call(x2d, vecs, wgu, wd, *, row_pre, tm, name):
    n = x2d.shape[0]
    return pl.pallas_call(
        functools.partial(_ffn_kernel, row_pre=row_pre),
        grid=(n // tm,),
        in_specs=[pl.BlockSpec((tm, D_MODEL), lambda i: (i, 0)),
                  _const_spec(vecs.shape), _const_spec(wgu.shape), _const_spec(wd.shape)],
        out_specs=pl.BlockSpec((tm, D_MODEL), lambda i: (i, 0)),
        out_shape=jax.ShapeDtypeStruct((n, D_MODEL), f32),
        compiler_params=pltpu.CompilerParams(dimension_semantics=("arbitrary",), vmem_limit_bytes=VMEM_LIMIT),
        name=name,
    )(x2d, vecs, wgu, wd)


def _gla_prepare(q, k, v, glog):
    rows = q.shape[0]
    c = GLA_CHUNK
    shift = c.bit_length() - 1
    ri = lax.broadcasted_iota(jnp.int32, (rows, rows), 0)
    ci = lax.broadcasted_iota(jnp.int32, (rows, rows), 1)
    mask = jnp.logical_and(lax.shift_right_logical(ri, shift) == lax.shift_right_logical(ci, shift), ri >= ci)
    ltri = mask.astype(bf16)
    g1 = glog.astype(bf16)
    r1 = glog - g1.astype(f32)
    g2 = r1.astype(bf16)
    g3 = (r1 - g2.astype(f32)).astype(bf16)
    bcum = _dot(ltri, g1) + _dot(ltri, g2) + _dot(ltri, g3)
    blast = [bcum[(n + 1) * c - 1:(n + 1) * c, :] for n in range(rows // c)]
    blast_b = jnp.concatenate([jnp.broadcast_to(b, (c, D_QK)) for b in blast], axis=0)
    qi = (q * (GLA_DK ** -0.5) * jnp.exp(bcum)).astype(bf16)
    ki = (k * jnp.exp(-bcum)).astype(bf16)
    kend = (k * jnp.exp(blast_b - bcum)).astype(bf16)
    dec = [jnp.exp(b) for b in blast]
    return qi, ki, kend, v.astype(bf16), dec, mask


def _gla_intra(qi, ki, vb, mask, h):
    ks = slice(h * GLA_DK, (h + 1) * GLA_DK)
    vs = slice(h * GLA_DV, (h + 1) * GLA_DV)
    att = lax.dot_general(qi[:, ks], ki[:, ks], (((1,), (1,)), ((), ())), preferred_element_type=f32)
    att = jnp.where(mask, att, 0.0).astype(bf16)
    return _dot(att, vb[:, vs])


def _gla_inter(qi, kend, vb, dec, S, h):
    c = GLA_CHUNK
    ks = slice(h * GLA_DK, (h + 1) * GLA_DK)
    vs = slice(h * GLA_DV, (h + 1) * GLA_DV)
    nch = qi.shape[0] // c
    upd, dcol = [], []
    for n in range(nch):
        rs = slice(n * c, (n + 1) * c)
        upd.append(lax.dot_general(kend[rs, ks], vb[rs, vs], (((0,), (0,)), ((), ())), preferred_element_type=f32))
        d = jnp.transpose(jnp.broadcast_to(dec[n][:, ks], (GLA_DK, GLA_DK)))
        dcol.append(jnp.concatenate([d, d], axis=1))
    sh = S[h]
    outs = []
    for n in range(nch):
        rs = slice(n * c, (n + 1) * c)
        outs.append(_dot(qi[rs, ks], sh.astype(bf16)))
        sh = sh * dcol[n] + upd[n]
    S[h] = sh
    return jnp.concatenate(outs, axis=0)


def _mix_prompt_kernel(x_ref, vec_ref, wmain_ref, wlr_ref, wlr2_ref, wg_ref, rgbd_ref, wbr_ref, wbg_ref, wo_ref,
                       y_ref, h_ref, conv_ref, s_ref,
                       xr_ext, hc, S, hbuf, *, tc):
    t = pl.program_id(1)
    nt = pl.num_programs(1)

    @pl.when(t == 0)
    def _():
        xr_ext[0:SUBLANES, :] = jnp.zeros((SUBLANES, D_RNN), f32)
        hc[...] = jnp.zeros_like(hc)
        S[...] = jnp.zeros_like(S)

    x = x_ref[...]
    u = _rms(x, vec_ref[R_NORM + 2:R_NORM + 3, :]).astype(bf16)

    xy = _dot(u, _w(wmain_ref[:, 0:2 * D_RNN]))
    xr = xy[:, :D_RNN]
    yr = xy[:, D_RNN:]
    xr_ext[SUBLANES:SUBLANES + tc, :] = xr

    qkvg = _dot(u, _w(wmain_ref[:, OFF_Q:OFF_LR]))

    xc = vec_ref[R_CONVB:R_CONVB + 1, :]
    for j in range(CONV_W):
        off = SUBLANES - (CONV_W - 1) + j
        xc = xc + xr_ext[off:off + tc, :] * vec_ref[R_CONVW + j:R_CONVW + j + 1, :]
    xr_ext[0:SUBLANES, :] = xr_ext[tc:tc + SUBLANES, :]

    glog = _decay_log(u, wlr_ref, wlr2_ref, vec_ref)
    ra, ri = _rglru_preact(xc, rgbd_ref)
    gates = _dot(u, _w(wg_ref[...]))

    a, mult, gi = _rglru_coeffs(ra, ri, vec_ref)
    row = lax.broadcasted_iota(jnp.int32, (tc, D_RNN), 0)
    mult = jnp.where(jnp.logical_and(row == 0, t == 0), 1.0, mult)
    b = mult * gi * xc
    rin = jnp.bitwise_and(row, SUBLANES - 1)
    s = 1
    while s < SUBLANES:
        a_sh = pltpu.roll(a, s, 0)
        b_sh = pltpu.roll(b, s, 0)
        m = rin >= s
        b = jnp.where(m, a * b_sh + b, b)
        a = jnp.where(m, a * a_sh, a)
        s *= 2

    blocks = []
    for r0 in range(0, tc, GLA_BLOCK):
        rs = slice(r0, r0 + GLA_BLOCK)
        blocks.append(_gla_prepare(qkvg[rs, 0:D_QK], qkvg[rs, D_QK:2 * D_QK], qkvg[rs, 2 * D_QK:2 * D_QK + D_V],
                                   glog[rs, :]))

    hprev = hc[...]
    for g in range(tc // SUBLANES):
        sl = slice(g * SUBLANES, (g + 1) * SUBLANES)
        hg = a[sl, :] * hprev + b[sl, :]
        hbuf[sl, :] = hg
        hprev = jnp.broadcast_to(hg[SUBLANES - 1:SUBLANES, :], (SUBLANES, D_RNN))
    hc[...] = hprev
    o_rnn = (hbuf[...] * _gelu_tanh(yr)).astype(bf16)

    ya = _dot(o_rnn, _w(wbr_ref[...]))

    og = qkvg[:, 2 * D_QK + D_V:]
    o_heads = []
    for h in range(GLA_HEADS):
        parts = []
        for r0, (qi, ki, kend, vb, dec, mask) in zip(range(0, tc, GLA_BLOCK), blocks):
            parts.append(_gla_intra(qi, ki, vb, mask, h) + _gla_inter(qi, kend, vb, dec, S, h))
        oh = parts[0] if len(parts) == 1 else jnp.concatenate(parts, axis=0)
        o_heads.append(_head_norm(oh, vec_ref, h) * _silu(og[:, h * GLA_DV:(h + 1) * GLA_DV]))
    part_a = _sigmoid(gates[:, :D_MODEL]) * ya
    o_gla = jnp.concatenate(o_heads, axis=1).astype(bf16)

    yb = _dot(o_gla, _w(wbg_ref[...]))
    merged = part_a + _sigmoid(gates[:, D_MODEL:]) * yb
    mix = _dot(merged.astype(bf16), _w(wo_ref[...]))
    y_ref[...] = x + _rms(mix, vec_ref[R_NORM + 3:R_NORM + 4, :])

    @pl.when(t == nt - 1)
    def _():
        h_ref[...] = hc[0:1, :]
        conv_ref[...] = xr_ext[SUBLANES - (CONV_W - 1):SUBLANES, :]
        s_ref[...] = S[...]


def _mix_prompt_call(x, vecs, wmain, wlr, wlr2, wg, rgbd, wbr, wbg, wo, *, tc):
    bsz, seq, _ = x.shape
    consts = (vecs, wmain, wlr, wlr2, wg, rgbd, wbr, wbg, wo)
    return pl.pallas_call(
        functools.partial(_mix_prompt_kernel, tc=tc),
        grid=(bsz, seq // tc),
        in_specs=[pl.BlockSpec((None, tc, D_MODEL), lambda b, t: (b, t, 0))] + [_const_spec(w.shape) for w in consts],
        out_specs=[pl.BlockSpec((None, tc, D_MODEL), lambda b, t: (b, t, 0)),
                   pl.BlockSpec((None, 1, D_RNN), lambda b, t: (b, 0, 0)),
                   pl.BlockSpec((None, CONV_W - 1, D_RNN), lambda b, t: (b, 0, 0)),
                   pl.BlockSpec((None, GLA_HEADS, GLA_DK, GLA_DV), lambda b, t: (b, 0, 0, 0))],
        out_shape=[jax.ShapeDtypeStruct((bsz, seq, D_MODEL), f32),
                   jax.ShapeDtypeStruct((bsz, 1, D_RNN), f32),
                   jax.ShapeDtypeStruct((bsz, CONV_W - 1, D_RNN), f32),
                   jax.ShapeDtypeStruct((bsz, GLA_HEADS, GLA_DK, GLA_DV), f32)],
        scratch_shapes=[pltpu.VMEM((tc + SUBLANES, D_RNN), f32),
                        pltpu.VMEM((SUBLANES, D_RNN), f32),
                        pltpu.VMEM((GLA_HEADS, GLA_DK, GLA_DV), f32),
                        pltpu.VMEM((tc, D_RNN), f32)],
        compiler_params=pltpu.CompilerParams(dimension_semantics=("arbitrary", "arbitrary"),
                                             vmem_limit_bytes=VMEM_LIMIT),
        name="mix_prompt",
    )(x, *consts)


def _sample_pre_kernel(x_ref, h0_ref, c0_ref, vec_ref, wgu_ref, wd_ref, wmain_ref, wlr_ref, wlr2_ref, wg_ref, rgbd_ref,
                       x1_ref, hn_ref, cn_ref, ornn_ref, qkvg_ref, glog_ref, gates_ref):
    x1 = _ffn(x_ref[...], vec_ref[R_NORM:R_NORM + 1, :], vec_ref[R_NORM + 1:R_NORM + 2, :], wgu_ref, wd_ref)
    x1_ref[...] = x1
    u = _rms(x1, vec_ref[R_NORM + 2:R_NORM + 3, :]).astype(bf16)
    xy = _dot(u, _w(wmain_ref[:, 0:2 * D_RNN]))
    xr = xy[:, :D_RNN]
    yr = xy[:, D_RNN:]
    xc = vec_ref[R_CONVB:R_CONVB + 1, :]
    for j in range(CONV_W - 1):
        xc = xc + c0_ref[:, j * D_RNN:(j + 1) * D_RNN] * vec_ref[R_CONVW + j:R_CONVW + j + 1, :]
    xc = xc + xr * vec_ref[R_CONVW + CONV_W - 1:R_CONVW + CONV_W, :]
    cn_ref[:, 0:(CONV_W - 2) * D_RNN] = c0_ref[:, D_RNN:(CONV_W - 1) * D_RNN]
    cn_ref[:, (CONV_W - 2) * D_RNN:] = xr
    ra, ri = _rglru_preact(xc, rgbd_ref)
    a, mult, gi = _rglru_coeffs(ra, ri, vec_ref)
    h = a * h0_ref[...] + mult * gi * xc
    hn_ref[...] = h
    ornn_ref[...] = (h * _gelu_tanh(yr)).astype(bf16)
    qkvg_ref[...] = _dot(u, _w(wmain_ref[:, OFF_Q:OFF_LR]))
    glog_ref[...] = _decay_log(u, wlr_ref, wlr2_ref, vec_ref)
    gates_ref[...] = _dot(u, _w(wg_ref[...]))


def _sample_pre_call(xs, h0, c0, vecs, wgu, wd, wmain, wlr, wlr2, wg, rgbd):
    n = xs.shape[0]
    args = (xs, h0, c0, vecs, wgu, wd, wmain, wlr, wlr2, wg, rgbd)
    outs = [((n, D_MODEL), f32), ((n, D_RNN), f32), ((n, (CONV_W - 1) * D_RNN), f32), ((n, D_RNN), bf16),
            ((n, OFF_LR - OFF_Q), f32), ((n, D_QK), f32), ((n, 2 * D_MODEL), f32)]
    return pl.pallas_call(
        _sample_pre_kernel,
        grid=(1,),
        in_specs=[_const_spec(a.shape) for a in args],
        out_specs=[pl.BlockSpec(s, lambda i: (0, 0)) for s, _ in outs],
        out_shape=[jax.ShapeDtypeStruct(s, d) for s, d in outs],
        compiler_params=pltpu.CompilerParams(dimension_semantics=("arbitrary",), vmem_limit_bytes=VMEM_LIMIT),
        name="sample_pre",
    )(*args)


def _col_bcast(rows8, width):
    tiled = jnp.concatenate([rows8] * (LANES // SUBLANES), axis=0)
    tt = jnp.transpose(tiled)
    return [jnp.broadcast_to(tt[:, j:j + 1], (LANES, width)) for j in range(SUBLANES)]


def _sample_gla_kernel(q_ref, k_ref, v_ref, g_ref, s0_ref, sn_ref, o_ref):
    scale = GLA_DK ** -0.5
    for h in range(GLA_HEADS):
        ks = slice(h * GLA_DK, (h + 1) * GLA_DK)
        vs = slice(h * GLA_DV, (h + 1) * GLA_DV)
        dcols = _col_bcast(jnp.exp(g_ref[:, ks]), GLA_DV)
        kcols = _col_bcast(k_ref[:, ks], GLA_DV)
        qcols = _col_bcast(q_ref[:, ks] * scale, GLA_DV)
        for j in range(SUBLANES):
            vrow = v_ref[j:j + 1, vs]
            sn = dcols[j] * s0_ref[j, h] + kcols[j] * vrow
            sn_ref[j, h] = sn
            o_ref[j:j + 1, vs] = jnp.sum(qcols[j] * sn, axis=0, keepdims=True)


def _sample_gla_call(q, k, v, g, s0):
    n = q.shape[0]
    sb = (SUBLANES, GLA_HEADS, GLA_DK, GLA_DV)
    return pl.pallas_call(
        _sample_gla_kernel,
        grid=(n // SUBLANES,),
        in_specs=[pl.BlockSpec((SUBLANES, D_QK), lambda i: (i, 0)),
                  pl.BlockSpec((SUBLANES, D_QK), lambda i: (i, 0)),
                  pl.BlockSpec((SUBLANES, D_V), lambda i: (i, 0)),
                  pl.BlockSpec((SUBLANES, D_QK), lambda i: (i, 0)),
                  pl.BlockSpec(sb, lambda i: (i, 0, 0, 0))],
        out_specs=[pl.BlockSpec(sb, lambda i: (i, 0, 0, 0)),
                   pl.BlockSpec((SUBLANES, D_V), lambda i: (i, 0))],
        out_shape=[jax.ShapeDtypeStruct(s0.shape, f32), jax.ShapeDtypeStruct((n, D_V), f32)],
        compiler_params=pltpu.CompilerParams(dimension_semantics=("arbitrary",), vmem_limit_bytes=VMEM_LIMIT),
        name="sample_gla",
    )(q, k, v, g, s0)


def _sample_post_kernel(x1_ref, ornn_ref, o_ref, og_ref, gates_ref, vec_ref, wbr_ref, wbg_ref, wo_ref, wgu_ref, wd_ref,
                        y_ref):
    o = o_ref[...]
    og = og_ref[...]
    o_gla = jnp.concatenate(
        [_head_norm(o[:, h * GLA_DV:(h + 1) * GLA_DV], vec_ref, h) * _silu(og[:, h * GLA_DV:(h + 1) * GLA_DV])
         for h in range(GLA_HEADS)], axis=1).astype(bf16)
    gates = gates_ref[...]
    ya = _dot(ornn_ref[...], _w(wbr_ref[...]))
    yb = _dot(o_gla, _w(wbg_ref[...]))
    merged = _sigmoid(gates[:, :D_MODEL]) * ya + _sigmoid(gates[:, D_MODEL:]) * yb
    mix = _dot(merged.astype(bf16), _w(wo_ref[...]))
    x2 = x1_ref[...] + _rms(mix, vec_ref[R_NORM + 3:R_NORM + 4, :])
    y_ref[...] = _ffn(x2, vec_ref[R_NORM + 4:R_NORM + 5, :], vec_ref[R_NORM + 5:R_NORM + 6, :], wgu_ref, wd_ref)


def _sample_post_call(x1, ornn, o, og, gates, vecs, wbr, wbg, wo, wgu, wd):
    n = x1.shape[0]
    args = (x1, ornn, o, og, gates, vecs, wbr, wbg, wo, wgu, wd)
    return pl.pallas_call(
        _sample_post_kernel,
        grid=(1,),
        in_specs=[_const_spec(a.shape) for a in args],
        out_specs=pl.BlockSpec((n, D_MODEL), lambda i: (0, 0)),
        out_shape=jax.ShapeDtypeStruct((n, D_MODEL), f32),
        compiler_params=pltpu.CompilerParams(dimension_semantics=("arbitrary",), vmem_limit_bytes=VMEM_LIMIT),
        name="sample_post",
    )(*args)


def _pack(w):
    return pltpu.bitcast(w.astype(bf16), jnp.uint32)


def _block_diag_gates(w_a, w_x):
    def bd(w):
        w = w.reshape(RNN_BLOCKS // 2, 2, RNN_BW, RNN_BW)
        z = jnp.zeros_like(w[:, 0])
        top = jnp.concatenate([w[:, 0], z], axis=2)
        bot = jnp.concatenate([z, w[:, 1]], axis=2)
        return jnp.concatenate([top, bot], axis=1)
    return jnp.concatenate([bd(w_a), bd(w_x)], axis=2)


def kernel(x_prompt, x_sample, state_rnn_h, state_rnn_conv, state_gla, norm_gains, ffn1_w_gu, ffn1_w_down, w_in, conv_w, conv_b, rg_w_a, rg_b_a, rg_w_x, rg_b_x, rg_lambda, gla_w_lr, gla_b_lr, gla_norm_g, w_branch_rnn, w_branch_gla, w_out, ffn2_w_gu, ffn2_w_down):
    assert w_in.shape == (1, D_MODEL, D_IN) and x_sample.shape[1] == 1
    bsz, seq, _ = x_prompt.shape
    nsmp = x_sample.shape[0]

    vecs = jnp.concatenate([
        norm_gains[0], conv_w[0], conv_b, rg_b_a, rg_b_x, rg_lambda,
        jnp.pad(gla_b_lr, ((0, 0), (0, D_MODEL - D_QK))), jnp.tile(gla_norm_g, (1, GLA_HEADS))], axis=0).astype(f32)
    w1gu = _pack(ffn1_w_gu[0])
    w1d = _pack(ffn1_w_down[0])
    w2gu = _pack(ffn2_w_gu[0])
    w2d = _pack(ffn2_w_down[0])
    wmain = _pack(w_in[0, :, :OFF_LR])
    wlr = _pack(jnp.pad(w_in[0, :, OFF_LR:OFF_GATES], ((0, 0), (0, LANES - GLA_RANK))))
    wlr2 = _pack(jnp.pad(gla_w_lr[0], ((0, LANES - GLA_RANK), (0, 0))))
    wg = _pack(w_in[0, :, OFF_GATES:])
    rgbd = _pack(_block_diag_gates(rg_w_a[0], rg_w_x[0]))
    wbr = _pack(w_branch_rnn[0])
    wbg = _pack(w_branch_gla[0])
    wo = _pack(w_out[0])

    xp = x_prompt.reshape(bsz * seq, D_MODEL)
    x1 = _ffn_call(xp, vecs, w1gu, w1d, row_pre=R_NORM, tm=512, name="ffn1_prompt")
    x2, hp, cp, sp = _mix_prompt_call(x1.reshape(bsz, seq, D_MODEL), vecs, wmain, wlr, wlr2, wg, rgbd, wbr, wbg, wo,
                                      tc=256)
    yp = _ffn_call(x2.reshape(bsz * seq, D_MODEL), vecs, w2gu, w2d, row_pre=R_NORM + 4, tm=512, name="ffn2_prompt")

    xs = x_sample.reshape(nsmp, D_MODEL)
    c0 = state_rnn_conv[0].reshape(nsmp, (CONV_W - 1) * D_RNN)
    x1s, hs, cs, ornn, qkvg, glog, gates = _sample_pre_call(
        xs, state_rnn_h[0], c0, vecs, w1gu, w1d, wmain, wlr, wlr2, wg, rgbd)
    ss, osmp = _sample_gla_call(qkvg[:, 0:D_QK], qkvg[:, D_QK:2 * D_QK], qkvg[:, 2 * D_QK:2 * D_QK + D_V], glog,
                                state_gla[0])
    ys = _sample_post_call(x1s, ornn, osmp, qkvg[:, 2 * D_QK + D_V:], gates, vecs, wbr, wbg, wo, w2gu, w2d)

    return (yp.reshape(bsz, seq, D_MODEL), ys.reshape(nsmp, 1, D_MODEL),
            hp.reshape(1, bsz, D_RNN), cp[None], sp[None],
            hs[None], cs.reshape(1, nsmp, CONV_W - 1, D_RNN), ss[None])
```

```python
import functools
import math

import jax
import jax.numpy as jnp
from jax import lax
from jax.experimental import pallas as pl
from jax.experimental.pallas import tpu as pltpu

D_MODEL = 1024
D_RNN = 1024
RNN_BLOCKS = 8
RNN_BW = D_RNN // RNN_BLOCKS
CONV_W = 4
RG_C = 8.0
GLA_HEADS = 4
GLA_DK = 128
GLA_DV = 256
GLA_RANK = 16
GLA_TAU = 16.0
GLA_CHUNK = 64
D_FF = 2816
EPS = 1e-6

D_QK = GLA_HEADS * GLA_DK
D_V = GLA_HEADS * GLA_DV
OFF_Q = 2 * D_RNN
OFF_K = OFF_Q + D_QK
OFF_V = OFF_K + D_QK
OFF_OG = OFF_V + D_V
OFF_LR = OFF_OG + D_V
OFF_GATES = OFF_LR + GLA_RANK
D_IN = OFF_GATES + 2 * D_MODEL

LANES = 128
SUBLANES = 8
VMEM_LIMIT = 56 * 1024 * 1024
PACK_COLS = 512
GLA_BLOCK = 256

R_NORM = 0
R_CONVW = 6
R_CONVB = 10
R_BA = 11
R_BX = 12
R_LAM = 13
R_BLR = 14
R_GNORM = 15

bf16 = jnp.bfloat16
f32 = jnp.float32


def _w(packed):
    return pltpu.bitcast(packed, bf16)


def _dot(a, b):
    return jnp.dot(a, b, preferred_element_type=f32)


def _rms(x, g):
    return x * lax.rsqrt(jnp.mean(x * x, axis=-1, keepdims=True) + EPS) * g


def _sigmoid(x):
    return 0.5 * jnp.tanh(0.5 * x) + 0.5


def _silu(x):
    hx = 0.5 * x
    return hx * jnp.tanh(hx) + hx


def _softplus(x):
    return jnp.maximum(x, 0.0) + jnp.log1p(jnp.exp(-jnp.abs(x)))


def _gelu_tanh(x):
    c = math.sqrt(2.0 / math.pi)
    return 0.5 * x * (1.0 + jnp.tanh(c * (x + 0.044715 * (x * x * x))))


def _ffn(x, g_pre, g_post, wgu_ref, wd_ref):
    u = _rms(x, g_pre).astype(bf16)
    gu = _dot(u, _w(wgu_ref[...]))
    h = (_silu(gu[:, :D_FF]) * gu[:, D_FF:]).astype(bf16)
    y = _dot(h, _w(wd_ref[...]))
    return x + 0.5 * _rms(y, g_post)


def _rglru_preact(xc, rgbd_ref):
    xcb = xc.astype(bf16)
    slab = 2 * RNN_BW
    ra, ri = [], []
    for s in range(D_RNN // slab):
        gi = _dot(xcb[:, s * slab:(s + 1) * slab], _w(rgbd_ref[s]))
        ra.append(gi[:, :slab])
        ri.append(gi[:, slab:])
    return jnp.concatenate(ra, axis=1), jnp.concatenate(ri, axis=1)


def _rglru_coeffs(ra, ri, vec_ref, cols=slice(None)):
    r = _sigmoid(ra + vec_ref[R_BA:R_BA + 1, cols])
    i = _sigmoid(ri + vec_ref[R_BX:R_BX + 1, cols])
    log_a = -RG_C * r * _softplus(-vec_ref[R_LAM:R_LAM + 1, cols])
    a = jnp.exp(log_a)
    mult = jnp.sqrt(jnp.maximum(-(jnp.tanh(log_a) * (a * a + 1.0)), 0.0))
    return a, mult, i


def _head_norm(oh, vec_ref, h):
    gh = vec_ref[R_GNORM:R_GNORM + 1, h * GLA_DV:(h + 1) * GLA_DV]
    return oh * lax.rsqrt(jnp.mean(oh * oh, axis=-1, keepdims=True) + EPS) * gh


def _decay_log(u, wlr_ref, wlr2_ref, vec_ref):
    lr = _dot(u, _w(wlr_ref[...]))
    z = _dot(lr.astype(bf16), _w(wlr2_ref[...])) + vec_ref[R_BLR:R_BLR + 1, 0:D_QK]
    return -_softplus(-z) * (1.0 / GLA_TAU)


def _ffn_kernel(x_ref, vec_ref, wgu_ref, wd_ref, o_ref, *, row_pre):
    o_ref[...] = _ffn(x_ref[...], vec_ref[row_pre:row_pre + 1, :], vec_ref[row_pre + 1:row_pre + 2, :],
                      wgu_ref, wd_ref)


def _const_spec(shape):
    nd = len(shape)
    return pl.BlockSpec(shape, lambda *_: (0,) * nd, pipeline_mode=pl.Buffered(1))


def _ffn_call(x2d, vecs, wgu, wd, *, row_pre, tm, name):
    n = x2d.shape[0]
    return pl.pallas_call(
        functools.partial(_ffn_kernel, row_pre=row_pre),
        grid=(n // tm,),
        in_specs=[pl.BlockSpec((tm, D_MODEL), lambda i: (i, 0)),
                  _const_spec(vecs.shape), _const_spec(wgu.shape), _const_spec(wd.shape)],
        out_specs=pl.BlockSpec((tm, D_MODEL), lambda i: (i, 0)),
        out_shape=jax.ShapeDtypeStruct((n, D_MODEL), f32),
        compiler_params=pltpu.CompilerParams(dimension_semantics=("arbitrary",), vmem_limit_bytes=VMEM_LIMIT),
        name=name,
    )(x2d, vecs, wgu, wd)


def _gla_prepare(q, k, v, glog):
    rows = q.shape[0]
    c = GLA_CHUNK
    shift = c.bit_length() - 1
    ri = lax.broadcasted_iota(jnp.int32, (rows, rows), 0)
    ci = lax.broadcasted_iota(jnp.int32, (rows, rows), 1)
    mask = jnp.logical_and(lax.shift_right_logical(ri, shift) == lax.shift_right_logical(ci, shift), ri >= ci)
    ltri = mask.astype(bf16)
    g1 = glog.astype(bf16)
    r1 = glog - g1.astype(f32)
    g2 = r1.astype(bf16)
    g3 = (r1 - g2.astype(f32)).astype(bf16)
    bcum = _dot(ltri, g1) + _dot(ltri, g2) + _dot(ltri, g3)
    blast = [bcum[(n + 1) * c - 1:(n + 1) * c, :] for n in range(rows // c)]
    blast_b = jnp.concatenate([jnp.broadcast_to(b, (c, D_QK)) for b in blast], axis=0)
    qi = (q * (GLA_DK ** -0.5) * jnp.exp(bcum)).astype(bf16)
    ki = (k * jnp.exp(-bcum)).astype(bf16)
    kend = (k * jnp.exp(blast_b - bcum)).astype(bf16)
    dec = [jnp.exp(b) for b in blast]
    return qi, ki, kend, v.astype(bf16), dec, mask


def _gla_intra(qi, ki, vb, mask, h):
    ks = slice(h * GLA_DK, (h + 1) * GLA_DK)
    vs = slice(h * GLA_DV, (h + 1) * GLA_DV)
    att = lax.dot_general(qi[:, ks], ki[:, ks], (((1,), (1,)), ((), ())), preferred_element_type=f32)
    att = jnp.where(mask, att, 0.0).astype(bf16)
    return _dot(att, vb[:, vs])


def _gla_inter(qi, kend, vb, dec, S, h, reset):
    c = GLA_CHUNK
    ks = slice(h * GLA_DK, (h + 1) * GLA_DK)
    vs = slice(h * GLA_DV, (h + 1) * GLA_DV)
    nch = qi.shape[0] // c
    upd, dcol = [], []
    for n in range(nch):
        rs = slice(n * c, (n + 1) * c)
        upd.append(lax.dot_general(kend[rs, ks], vb[rs, vs], (((0,), (0,)), ((), ())), preferred_element_type=f32))
        d = jnp.transpose(jnp.broadcast_to(dec[n][:, ks], (GLA_DK, GLA_DK)))
        dcol.append(jnp.concatenate([d, d], axis=1))
    sh = jnp.where(reset, 0.0, S[h])
    outs = []
    for n in range(nch):
        rs = slice(n * c, (n + 1) * c)
        outs.append(_dot(qi[rs, ks], sh.astype(bf16)))
        sh = sh * dcol[n] + upd[n]
    S[h] = sh
    return jnp.concatenate(outs, axis=0)


def _interleave(*gens):
    live = list(gens)
    while live:
        for g in list(live):
            try:
                next(g)
            except StopIteration:
                live.remove(g)


def _mix_prompt_kernel(xp_ref, xa_ref, vec_ref, wmain_ref, wlr_ref, wlr2_ref, wg_ref, rgbd_ref, wbr_ref, wbg_ref,
                       wo_ref, y_ref, h_ref, conv_ref, s_ref,
                       u_s, xrb, yr_s, ornn_s, hc, tail, S, hbuf, *, bpb):
    j = pl.program_id(0)
    rb = GLA_BLOCK
    sw = 2 * RNN_BW
    nw = 512

    @pl.when(j == 0)
    def _():
        u_s[...] = jnp.zeros_like(u_s)
        xrb[...] = jnp.zeros_like(xrb)
        yr_s[...] = jnp.zeros_like(yr_s)
        ornn_s[...] = jnp.zeros_like(ornn_s)
        hc[...] = jnp.zeros_like(hc)
        tail[...] = jnp.zeros_like(tail)
        S[...] = jnp.zeros_like(S)

    p3 = j % 3
    p2 = j % 2
    r2 = (j + 1) % 2
    a3 = (j + 1) % 3
    r_first = (j + bpb - 1) % bpb == 0
    a_first = (j + 2 * bpb - 2) % bpb == 0

    def project():
        u = _rms(xp_ref[...], vec_ref[R_NORM + 2:R_NORM + 3, :]).astype(bf16)
        u_s[p3] = u
        yield
        for c0 in range(0, D_RNN, sw):
            xrb[p2, SUBLANES:SUBLANES + rb, c0:c0 + sw] = _dot(u, _w(wmain_ref[:, c0:c0 + sw]))
            yield
        for c0 in range(0, D_RNN, nw):
            yr_s[p2, :, c0:c0 + nw] = _dot(u, _w(wmain_ref[:, D_RNN + c0:D_RNN + c0 + nw]))
            yield

    def recur():
        xrb[r2, 0:SUBLANES, :] = jnp.where(r_first, 0.0, tail[...])
        row = lax.broadcasted_iota(jnp.int32, (rb, sw), 0)
        rin = jnp.bitwise_and(row, SUBLANES - 1)
        for s in range(D_RNN // sw):
            cs = slice(s * sw, (s + 1) * sw)
            xc = vec_ref[R_CONVB:R_CONVB + 1, cs]
            for t in range(CONV_W):
                off = SUBLANES - (CONV_W - 1) + t
                xc = xc + xrb[r2, off:off + rb, cs] * vec_ref[R_CONVW + t:R_CONVW + t + 1, cs]
            yield
            pre = _dot(xc.astype(bf16), _w(rgbd_ref[s]))
            a, mult, gi = _rglru_coeffs(pre[:, :sw], pre[:, sw:], vec_ref, cs)
            mult = jnp.where(jnp.logical_and(row == 0, r_first), 1.0, mult)
            b = mult * gi * xc
            yield
            sh = 1
            while sh < SUBLANES:
                a_sh = pltpu.roll(a, sh, 0)
                b_sh = pltpu.roll(b, sh, 0)
                m = rin >= sh
                b = jnp.where(m, a * b_sh + b, b)
                a = jnp.where(m, a * a_sh, a)
                sh *= 2
            yield
            hprev = jnp.where(r_first, 0.0, hc[:, cs])
            for g in range(rb // SUBLANES):
                sl = slice(g * SUBLANES, (g + 1) * SUBLANES)
                hg = a[sl, :] * hprev + b[sl, :]
                hbuf[sl, cs] = hg
                hprev = jnp.broadcast_to(hg[SUBLANES - 1:SUBLANES, :], (SUBLANES, sw))
            hc[:, cs] = hprev
            ornn_s[r2, :, cs] = (hbuf[:, cs] * _gelu_tanh(yr_s[r2, :, cs])).astype(bf16)
            yield
        tail[...] = xrb[r2, rb:rb + SUBLANES, :]

    def attend():
        u = u_s[a3]
        proj = []
        for c0 in range(OFF_Q, OFF_LR, nw):
            proj.append(_dot(u, _w(wmain_ref[:, c0:c0 + nw])))
            yield
        q, k = proj[0], proj[1]
        v = jnp.concatenate(proj[2:4], axis=1)
        og = jnp.concatenate(proj[4:6], axis=1)
        glog = _decay_log(u, wlr_ref, wlr2_ref, vec_ref)
        yield
        gates = []
        for c0 in range(0, 2 * D_MODEL, nw):
            gates.append(_dot(u, _w(wg_ref[:, c0:c0 + nw])))
            yield
        qi, ki, kend, vb, dec, mask = _gla_prepare(q, k, v, glog)
        yield
        o_rnn = ornn_s[p2]
        part_a = []
        for i, c0 in enumerate(range(0, D_MODEL, nw)):
            part_a.append(_sigmoid(gates[i]) * _dot(o_rnn, _w(wbr_ref[:, c0:c0 + nw])))
            yield
        o_heads = []
        for h in range(GLA_HEADS):
            oh = _gla_intra(qi, ki, vb, mask, h) + _gla_inter(qi, kend, vb, dec, S, h, a_first)
            o_heads.append((_head_norm(oh, vec_ref, h) * _silu(og[:, h * GLA_DV:(h + 1) * GLA_DV])).astype(bf16))
            yield
        o_gla = jnp.concatenate(o_heads, axis=1)
        merged = []
        ng = D_MODEL // nw
        for i, c0 in enumerate(range(0, D_MODEL, nw)):
            yb = _dot(o_gla, _w(wbg_ref[:, c0:c0 + nw]))
            merged.append((part_a[i] + _sigmoid(gates[ng + i]) * yb).astype(bf16))
            yield
        merged = jnp.concatenate(merged, axis=1)
        mix = []
        for c0 in range(0, D_MODEL, nw):
            mix.append(_dot(merged, _w(wo_ref[:, c0:c0 + nw])))
            yield
        mix = jnp.concatenate(mix, axis=1)
        y_ref[...] = xa_ref[...] + _rms(mix, vec_ref[R_NORM + 3:R_NORM + 4, :])
        yield

    _interleave(recur(), attend(), project())

    @pl.when((j + bpb - 1) % bpb == bpb - 1)
    def _():
        h_ref[...] = hc[0:1, :]
        conv_ref[...] = tail[SUBLANES - (CONV_W - 1):SUBLANES, :]

    @pl.when((j + 2 * bpb - 2) % bpb == bpb - 1)
    def _():
        s_ref[...] = S[...]


def _mix_prompt_call(x2d, vecs, wmain, wlr, wlr2, wg, rgbd, wbr, wbg, wo, *, bsz, seq):
    rb = GLA_BLOCK
    bpb = seq // rb
    nblk = bsz * bpb
    consts = (vecs, wmain, wlr, wlr2, wg, rgbd, wbr, wbg, wo)

    def blk(d):
        return lambda j: (jnp.clip(j - d, 0, nblk - 1), 0)

    def seq_of(d, nd):
        return lambda j: (jnp.clip(j - d, 0, nblk - 1) // bpb,) + (0,) * nd

    return pl.pallas_call(
        functools.partial(_mix_prompt_kernel, bpb=bpb),
        grid=(nblk + 2,),
        in_specs=[pl.BlockSpec((rb, D_MODEL), blk(0)), pl.BlockSpec((rb, D_MODEL), blk(2))]
        + [_const_spec(w.shape) for w in consts],
        out_specs=[pl.BlockSpec((rb, D_MODEL), blk(2)),
                   pl.BlockSpec((None, 1, D_RNN), seq_of(1, 2)),
                   pl.BlockSpec((None, CONV_W - 1, D_RNN), seq_of(1, 2)),
                   pl.BlockSpec((None, GLA_HEADS, GLA_DK, GLA_DV), seq_of(2, 3))],
        out_shape=[jax.ShapeDtypeStruct((nblk * rb, D_MODEL), f32),
                   jax.ShapeDtypeStruct((bsz, 1, D_RNN), f32),
                   jax.ShapeDtypeStruct((bsz, CONV_W - 1, D_RNN), f32),
                   jax.ShapeDtypeStruct((bsz, GLA_HEADS, GLA_DK, GLA_DV), f32)],
        scratch_shapes=[pltpu.VMEM((3, rb, D_MODEL), bf16),
                        pltpu.VMEM((2, rb + SUBLANES, D_RNN), f32),
                        pltpu.VMEM((2, rb, D_RNN), f32),
                        pltpu.VMEM((2, rb, D_RNN), bf16),
                        pltpu.VMEM((SUBLANES, D_RNN), f32),
                        pltpu.VMEM((SUBLANES, D_RNN), f32),
                        pltpu.VMEM((GLA_HEADS, GLA_DK, GLA_DV), f32),
                        pltpu.VMEM((rb, D_RNN), f32)],
        compiler_params=pltpu.CompilerParams(dimension_semantics=("arbitrary",), vmem_limit_bytes=VMEM_LIMIT),
        name="mix_prompt",
    )(x2d, x2d, *consts)


def _sample_pre_kernel(x_ref, h0_ref, c0_ref, vec_ref, wgu_ref, wd_ref, wmain_ref, wlr_ref, wlr2_ref, wg_ref, rgbd_ref,
                       x1_ref, hn_ref, cn_ref, ornn_ref, qkvg_ref, glog_ref, gates_ref):
    x1 = _ffn(x_ref[...], vec_ref[R_NORM:R_NORM + 1, :], vec_ref[R_NORM + 1:R_NORM + 2, :], wgu_ref, wd_ref)
    x1_ref[...] = x1
    u = _rms(x1, vec_ref[R_NORM + 2:R_NORM + 3, :]).astype(bf16)
    xy = _dot(u, _w(wmain_ref[:, 0:2 * D_RNN]))
    xr = xy[:, :D_RNN]
    yr = xy[:, D_RNN:]
    xc = vec_ref[R_CONVB:R_CONVB + 1, :]
    for j in range(CONV_W - 1):
        xc = xc + c0_ref[:, j * D_RNN:(j + 1) * D_RNN] * vec_ref[R_CONVW + j:R_CONVW + j + 1, :]
    xc = xc + xr * vec_ref[R_CONVW + CONV_W - 1:R_CONVW + CONV_W, :]
    cn_ref[:, 0:(CONV_W - 2) * D_RNN] = c0_ref[:, D_RNN:(CONV_W - 1) * D_RNN]
    cn_ref[:, (CONV_W - 2) * D_RNN:] = xr
    ra, ri = _rglru_preact(xc, rgbd_ref)
    a, mult, gi = _rglru_coeffs(ra, ri, vec_ref)
    h = a * h0_ref[...] + mult * gi * xc
    hn_ref[...] = h
    ornn_ref[...] = (h * _gelu_tanh(yr)).astype(bf16)
    qkvg_ref[...] = _dot(u, _w(wmain_ref[:, OFF_Q:OFF_LR]))
    glog_ref[...] = _decay_log(u, wlr_ref, wlr2_ref, vec_ref)
    gates_ref[...] = _dot(u, _w(wg_ref[...]))


def _sample_pre_call(xs, h0, c0, vecs, wgu, wd, wmain, wlr, wlr2, wg, rgbd):
    n = xs.shape[0]
    args = (xs, h0, c0, vecs, wgu, wd, wmain, wlr, wlr2, wg, rgbd)
    outs = [((n, D_MODEL), f32), ((n, D_RNN), f32), ((n, (CONV_W - 1) * D_RNN), f32), ((n, D_RNN), bf16),
            ((n, OFF_LR - OFF_Q), f32), ((n, D_QK), f32), ((n, 2 * D_MODEL), f32)]
    return pl.pallas_call(
        _sample_pre_kernel,
        grid=(1,),
        in_specs=[_const_spec(a.shape) for a in args],
        out_specs=[pl.BlockSpec(s, lambda i: (0, 0)) for s, _ in outs],
        out_shape=[jax.ShapeDtypeStruct(s, d) for s, d in outs],
        compiler_params=pltpu.CompilerParams(dimension_semantics=("arbitrary",), vmem_limit_bytes=VMEM_LIMIT),
        name="sample_pre",
    )(*args)


def _col_bcast(rows8, width):
    tiled = jnp.concatenate([rows8] * (LANES // SUBLANES), axis=0)
    tt = jnp.transpose(tiled)
    return [jnp.broadcast_to(tt[:, j:j + 1], (LANES, width)) for j in range(SUBLANES)]


def _sample_gla_kernel(q_ref, k_ref, v_ref, g_ref, s0_ref, sn_ref, o_ref):
    scale = GLA_DK ** -0.5
    for h in range(GLA_HEADS):
        ks = slice(h * GLA_DK, (h + 1) * GLA_DK)
        vs = slice(h * GLA_DV, (h + 1) * GLA_DV)
        dcols = _col_bcast(jnp.exp(g_ref[:, ks]), GLA_DV)
        kcols = _col_bcast(k_ref[:, ks], GLA_DV)
        qcols = _col_bcast(q_ref[:, ks] * scale, GLA_DV)
        for j in range(SUBLANES):
            vrow = v_ref[j:j + 1, vs]
            sn = dcols[j] * s0_ref[j, h] + kcols[j] * vrow
            sn_ref[j, h] = sn
            o_ref[j:j + 1, vs] = jnp.sum(qcols[j] * sn, axis=0, keepdims=True)


def _sample_gla_call(q, k, v, g, s0):
    n = q.shape[0]
    sb = (SUBLANES, GLA_HEADS, GLA_DK, GLA_DV)
    return pl.pallas_call(
        _sample_gla_kernel,
        grid=(n // SUBLANES,),
        in_specs=[pl.BlockSpec((SUBLANES, D_QK), lambda i: (i, 0)),
                  pl.BlockSpec((SUBLANES, D_QK), lambda i: (i, 0)),
                  pl.BlockSpec((SUBLANES, D_V), lambda i: (i, 0)),
                  pl.BlockSpec((SUBLANES, D_QK), lambda i: (i, 0)),
                  pl.BlockSpec(sb, lambda i: (i, 0, 0, 0))],
        out_specs=[pl.BlockSpec(sb, lambda i: (i, 0, 0, 0)),
                   pl.BlockSpec((SUBLANES, D_V), lambda i: (i, 0))],
        out_shape=[jax.ShapeDtypeStruct(s0.shape, f32), jax.ShapeDtypeStruct((n, D_V), f32)],
        compiler_params=pltpu.CompilerParams(dimension_semantics=("arbitrary",), vmem_limit_bytes=VMEM_LIMIT),
        name="sample_gla",
    )(q, k, v, g, s0)


def _sample_post_kernel(x1_ref, ornn_ref, o_ref, og_ref, gates_ref, vec_ref, wbr_ref, wbg_ref, wo_ref, wgu_ref, wd_ref,
                        y_ref):
    o = o_ref[...]
    og = og_ref[...]
    o_gla = jnp.concatenate(
        [_head_norm(o[:, h * GLA_DV:(h + 1) * GLA_DV], vec_ref, h) * _silu(og[:, h * GLA_DV:(h + 1) * GLA_DV])
         for h in range(GLA_HEADS)], axis=1).astype(bf16)
    gates = gates_ref[...]
    ya = _dot(ornn_ref[...], _w(wbr_ref[...]))
    yb = _dot(o_gla, _w(wbg_ref[...]))
    merged = _sigmoid(gates[:, :D_MODEL]) * ya + _sigmoid(gates[:, D_MODEL:]) * yb
    mix = _dot(merged.astype(bf16), _w(wo_ref[...]))
    x2 = x1_ref[...] + _rms(mix, vec_ref[R_NORM + 3:R_NORM + 4, :])
    y_ref[...] = _ffn(x2, vec_ref[R_NORM + 4:R_NORM + 5, :], vec_ref[R_NORM + 5:R_NORM + 6, :], wgu_ref, wd_ref)


def _sample_post_call(x1, ornn, o, og, gates, vecs, wbr, wbg, wo, wgu, wd):
    n = x1.shape[0]
    args = (x1, ornn, o, og, gates, vecs, wbr, wbg, wo, wgu, wd)
    return pl.pallas_call(
        _sample_post_kernel,
        grid=(1,),
        in_specs=[_const_spec(a.shape) for a in args],
        out_specs=pl.BlockSpec((n, D_MODEL), lambda i: (0, 0)),
        out_shape=jax.ShapeDtypeStruct((n, D_MODEL), f32),
        compiler_params=pltpu.CompilerParams(dimension_semantics=("arbitrary",), vmem_limit_bytes=VMEM_LIMIT),
        name="sample_post",
    )(*args)


def _pack_kernel(w_ref, o_ref):
    o_ref[...] = pltpu.bitcast(w_ref[...].astype(bf16), jnp.uint32)


def _pack(w, *, col0=0, ncols=None, name):
    nl, k, n = w.shape
    ncols = n if ncols is None else ncols
    bn = PACK_COLS if ncols % PACK_COLS == 0 else ncols
    assert col0 % bn == 0 and ncols % bn == 0
    return pl.pallas_call(
        _pack_kernel,
        grid=(nl, ncols // bn),
        in_specs=[pl.BlockSpec((None, k, bn), lambda l, j: (l, 0, j + col0 // bn))],
        out_specs=pl.BlockSpec((None, k // 2, bn), lambda l, j: (l, 0, j)),
        out_shape=jax.ShapeDtypeStruct((nl, k // 2, ncols), jnp.uint32),
        compiler_params=pltpu.CompilerParams(dimension_semantics=("arbitrary", "arbitrary"),
                                             vmem_limit_bytes=VMEM_LIMIT),
        name=name,
    )(w)


def _block_diag_gates(w_a, w_x):
    def bd(w):
        w = w.reshape(RNN_BLOCKS // 2, 2, RNN_BW, RNN_BW)
        z = jnp.zeros_like(w[:, 0])
        top = jnp.concatenate([w[:, 0], z], axis=2)
        bot = jnp.concatenate([z, w[:, 1]], axis=2)
        return jnp.concatenate([top, bot], axis=1)
    return jnp.concatenate([bd(w_a), bd(w_x)], axis=2)


def kernel(x_prompt, x_sample, state_rnn_h, state_rnn_conv, state_gla, norm_gains, ffn1_w_gu, ffn1_w_down, w_in, conv_w, conv_b, rg_w_a, rg_b_a, rg_w_x, rg_b_x, rg_lambda, gla_w_lr, gla_b_lr, gla_norm_g, w_branch_rnn, w_branch_gla, w_out, ffn2_w_gu, ffn2_w_down):
    assert w_in.shape == (1, D_MODEL, D_IN) and x_sample.shape[1] == 1
    bsz, seq, _ = x_prompt.shape
    nsmp = x_sample.shape[0]

    vecs = jnp.concatenate([
        norm_gains[0], conv_w[0], conv_b, rg_b_a, rg_b_x, rg_lambda,
        jnp.pad(gla_b_lr, ((0, 0), (0, D_MODEL - D_QK))), jnp.tile(gla_norm_g, (1, GLA_HEADS))], axis=0).astype(f32)
    w1gu = _pack(ffn1_w_gu, name="pack_w1gu")[0]
    w1d = _pack(ffn1_w_down, name="pack_w1d")[0]
    w2gu = _pack(ffn2_w_gu, name="pack_w2gu")[0]
    w2d = _pack(ffn2_w_down, name="pack_w2d")[0]
    wmain = _pack(w_in, ncols=OFF_LR, name="pack_wmain")[0]
    wlr = _pack(jnp.pad(w_in[:, :, OFF_LR:OFF_GATES], ((0, 0), (0, 0), (0, LANES - GLA_RANK))), name="pack_wlr")[0]
    wlr2 = _pack(jnp.pad(gla_w_lr, ((0, 0), (0, LANES - GLA_RANK), (0, 0))), name="pack_wlr2")[0]
    wg = _pack(w_in[:, :, OFF_GATES:], name="pack_wg")[0]
    rgbd = _pack(_block_diag_gates(rg_w_a[0], rg_w_x[0]), name="pack_rgbd")
    wbr = _pack(w_branch_rnn, name="pack_wbr")[0]
    wbg = _pack(w_branch_gla, name="pack_wbg")[0]
    wo = _pack(w_out, name="pack_wo")[0]

    xp = x_prompt.reshape(bsz * seq, D_MODEL)
    x1 = _ffn_call(xp, vecs, w1gu, w1d, row_pre=R_NORM, tm=512, name="ffn1_prompt")
    x2, hp, cp, sp = _mix_prompt_call(x1, vecs, wmain, wlr, wlr2, wg, rgbd, wbr, wbg, wo, bsz=bsz, seq=seq)
    yp = _ffn_call(x2, vecs, w2gu, w2d, row_pre=R_NORM + 4, tm=512, name="ffn2_prompt")

    xs = x_sample.reshape(nsmp, D_MODEL)
    c0 = state_rnn_conv[0].reshape(nsmp, (CONV_W - 1) * D_RNN)
    x1s, hs, cs, ornn, qkvg, glog, gates = _sample_pre_call(
        xs, state_rnn_h[0], c0, vecs, w1gu, w1d, wmain, wlr, wlr2, wg, rgbd)
    ss, osmp = _sample_gla_call(qkvg[:, 0:D_QK], qkvg[:, D_QK:2 * D_QK], qkvg[:, 2 * D_QK:2 * D_QK + D_V], glog,
                                state_gla[0])
    ys = _sample_post_call(x1s, ornn, osmp, qkvg[:, 2 * D_QK + D_V:], gates, vecs, wbr, wbg, wo, w2gu, w2d)

    return (yp.reshape(bsz, seq, D_MODEL), ys.reshape(nsmp, 1, D_MODEL),
            hp.reshape(1, bsz, D_RNN), cp[None], sp[None],
            hs[None], cs.reshape(1, nsmp, CONV_W - 1, D_RNN), ss[None])
```

```python
import functools
import math

import jax
import jax.numpy as jnp
from jax import lax
from jax.experimental import pallas as pl
from jax.experimental.pallas import tpu as pltpu

D_MODEL = 1024
D_RNN = 1024
RNN_BLOCKS = 8
RNN_BW = D_RNN // RNN_BLOCKS
CONV_W = 4
RG_C = 8.0
GLA_HEADS = 4
GLA_DK = 128
GLA_DV = 256
GLA_RANK = 16
GLA_TAU = 16.0
GLA_CHUNK = 64
D_FF = 2816
EPS = 1e-6

D_QK = GLA_HEADS * GLA_DK
D_V = GLA_HEADS * GLA_DV
OFF_Q = 2 * D_RNN
OFF_K = OFF_Q + D_QK
OFF_V = OFF_K + D_QK
OFF_OG = OFF_V + D_V
OFF_LR = OFF_OG + D_V
OFF_GATES = OFF_LR + GLA_RANK
D_IN = OFF_GATES + 2 * D_MODEL

LANES = 128
SUBLANES = 8
VMEM_LIMIT = 56 * 1024 * 1024
PACK_COLS = 512
ROW_CHUNK = 32
GLA_BLOCK = 256

R_NORM = 0
R_CONVW = 6
R_CONVB = 10
R_BA = 11
R_BX = 12
R_LAM = 13
R_BLR = 14
R_GNORM = 15

bf16 = jnp.bfloat16
f32 = jnp.float32


def _w(packed):
    return pltpu.bitcast(packed, bf16)


def _dot(a, b):
    return jnp.dot(a, b, preferred_element_type=f32)


def _rms(x, g):
    return x * lax.rsqrt(jnp.mean(x * x, axis=-1, keepdims=True) + EPS) * g


def _sigmoid(x):
    return 0.5 * jnp.tanh(0.5 * x) + 0.5


def _silu(x):
    hx = 0.5 * x
    return hx * jnp.tanh(hx) + hx


def _softplus(x):
    return jnp.maximum(x, 0.0) + jnp.log1p(jnp.exp(-jnp.abs(x)))


def _sqrt_nonneg(m):
    return jnp.where(m > 0.0, m * lax.rsqrt(m), 0.0)


def _gelu_tanh(x):
    c = math.sqrt(2.0 / math.pi)
    hx = 0.5 * x
    return hx + hx * jnp.tanh(x * (c + (0.044715 * c) * (x * x)))


def _ffn(x, g_pre, g_post, wgu_ref, wd_ref):
    u = _rms(x, g_pre).astype(bf16)
    gu = _dot(u, _w(wgu_ref[...]))
    h = (_silu(gu[:, :D_FF]) * gu[:, D_FF:]).astype(bf16)
    y = _dot(h, _w(wd_ref[...]))
    return x + 0.5 * _rms(y, g_post)


def _rglru_preact(xc, rgbd_ref):
    xcb = xc.astype(bf16)
    slab = 2 * RNN_BW
    ra, ri = [], []
    for s in range(D_RNN // slab):
        gi = _dot(xcb[:, s * slab:(s + 1) * slab], _w(rgbd_ref[s]))
        ra.append(gi[:, :slab])
        ri.append(gi[:, slab:])
    return jnp.concatenate(ra, axis=1), jnp.concatenate(ri, axis=1)


def _rglru_coeffs(ra, ri, vec_ref, cols=slice(None)):
    r = _sigmoid(ra + vec_ref[R_BA:R_BA + 1, cols])
    i = _sigmoid(ri + vec_ref[R_BX:R_BX + 1, cols])
    log_a = -RG_C * r * _softplus(-vec_ref[R_LAM:R_LAM + 1, cols])
    a = jnp.exp(log_a)
    mult = _sqrt_nonneg(jnp.maximum(-(jnp.tanh(log_a) * (a * a + 1.0)), 0.0))
    return a, mult, i


def _head_norm(oh, vec_ref, h):
    gh = vec_ref[R_GNORM:R_GNORM + 1, h * GLA_DV:(h + 1) * GLA_DV]
    return oh * lax.rsqrt(jnp.mean(oh * oh, axis=-1, keepdims=True) + EPS) * gh


def _decay_log(u, wlr_ref, wlr2_ref, vec_ref):
    lr = _dot(u, _w(wlr_ref[...]))
    z = _dot(lr.astype(bf16), _w(wlr2_ref[...])) + vec_ref[R_BLR:R_BLR + 1, 0:D_QK]
    softplus_neg = jnp.maximum(-z, 0.0) + jnp.log(1.0 + jnp.exp(-jnp.abs(z)))
    return -softplus_neg * (1.0 / GLA_TAU)


def _ffn_kernel(x_ref, vec_ref, wgu_ref, wd_ref, o_ref, *, row_pre):
    o_ref[...] = _ffn(x_ref[...], vec_ref[row_pre:row_pre + 1, :], vec_ref[row_pre + 1:row_pre + 2, :],
                      wgu_ref, wd_ref)


def _const_spec(shape):
    nd = len(shape)
    return pl.BlockSpec(shape, lambda *_: (0,) * nd, pipeline_mode=pl.Buffered(1))


def _ffn_call(x2d, vecs, wgu, wd, *, row_pre, tm, name):
    n = x2d.shape[0]
    return pl.pallas_call(
        functools.partial(_ffn_kernel, row_pre=row_pre),
        grid=(n // tm,),
        in_specs=[pl.BlockSpec((tm, D_MODEL), lambda i: (i, 0)),
                  _const_spec(vecs.shape), _const_spec(wgu.shape), _const_spec(wd.shape)],
        out_specs=pl.BlockSpec((tm, D_MODEL), lambda i: (i, 0)),
        out_shape=jax.ShapeDtypeStruct((n, D_MODEL), f32),
        compiler_params=pltpu.CompilerParams(dimension_semantics=("arbitrary",), vmem_limit_bytes=VMEM_LIMIT),
        name=name,
    )(x2d, vecs, wgu, wd)


def _gla_prepare(q, k, v, glog):
    rows = q.shape[0]
    c = GLA_CHUNK
    shift = c.bit_length() - 1
    ri = lax.broadcasted_iota(jnp.int32, (rows, rows), 0)
    ci = lax.broadcasted_iota(jnp.int32, (rows, rows), 1)
    mask = jnp.logical_and(lax.shift_right_logical(ri, shift) == lax.shift_right_logical(ci, shift), ri >= ci)
    ltri = mask.astype(bf16)
    g1 = glog.astype(bf16)
    r1 = glog - g1.astype(f32)
    g2 = r1.astype(bf16)
    g3 = (r1 - g2.astype(f32)).astype(bf16)
    bcum = _dot(ltri, g1) + _dot(ltri, g2) + _dot(ltri, g3)
    blast = [bcum[(n + 1) * c - 1:(n + 1) * c, :] for n in range(rows // c)]
    blast_b = jnp.concatenate([jnp.broadcast_to(b, (c, D_QK)) for b in blast], axis=0)
    qi = (q * (GLA_DK ** -0.5) * jnp.exp(bcum)).astype(bf16)
    ki = (k * jnp.exp(-bcum)).astype(bf16)
    kend = (k * jnp.exp(blast_b - bcum)).astype(bf16)
    dec = [jnp.exp(b) for b in blast]
    return qi, ki, kend, v.astype(bf16), dec, mask


def _gla_intra(qi, ki, vb, mask, h):
    ks = slice(h * GLA_DK, (h + 1) * GLA_DK)
    vs = slice(h * GLA_DV, (h + 1) * GLA_DV)
    att = lax.dot_general(qi[:, ks], ki[:, ks], (((1,), (1,)), ((), ())), preferred_element_type=f32)
    att = jnp.where(mask, att, 0.0).astype(bf16)
    return _dot(att, vb[:, vs])


def _gla_inter(qi, kend, vb, dec, S, h, reset):
    c = GLA_CHUNK
    ks = slice(h * GLA_DK, (h + 1) * GLA_DK)
    vs = slice(h * GLA_DV, (h + 1) * GLA_DV)
    nch = qi.shape[0] // c
    upd, dcol = [], []
    for n in range(nch):
        rs = slice(n * c, (n + 1) * c)
        upd.append(lax.dot_general(kend[rs, ks], vb[rs, vs], (((0,), (0,)), ((), ())), preferred_element_type=f32))
        d = jnp.transpose(jnp.broadcast_to(dec[n][:, ks], (GLA_DK, GLA_DK)))
        dcol.append(jnp.concatenate([d, d], axis=1))
    sh = jnp.where(reset, 0.0, S[h])
    outs = []
    for n in range(nch):
        rs = slice(n * c, (n + 1) * c)
        outs.append(_dot(qi[rs, ks], sh.astype(bf16)))
        sh = sh * dcol[n] + upd[n]
    S[h] = sh
    return jnp.concatenate(outs, axis=0)


def _interleave(*gens):
    live = list(gens)
    while live:
        for g in list(live):
            try:
                next(g)
            except StopIteration:
                live.remove(g)


def _mix_prompt_kernel(xp_ref, xa_ref, vec_ref, wmain_ref, wlr_ref, wlr2_ref, wg_ref, rgbd_ref, wbr_ref, wbg_ref,
                       wo_ref, y_ref, h_ref, conv_ref, s_ref,
                       u_s, xrb, yr_s, ornn_s, hc, tail, S, hbuf, pre_s, *, bpb):
    j = pl.program_id(0)
    rb = GLA_BLOCK
    sw = 2 * RNN_BW
    nw = 512

    @pl.when(j == 0)
    def _():
        u_s[...] = jnp.zeros_like(u_s)
        xrb[...] = jnp.zeros_like(xrb)
        yr_s[...] = jnp.zeros_like(yr_s)
        ornn_s[...] = jnp.zeros_like(ornn_s)
        hc[...] = jnp.zeros_like(hc)
        tail[...] = jnp.zeros_like(tail)
        S[...] = jnp.zeros_like(S)

    p3 = j % 3
    p2 = j % 2
    r2 = (j + 1) % 2
    a3 = (j + 1) % 3
    r_first = (j + bpb - 1) % bpb == 0
    a_first = (j + 2 * bpb - 2) % bpb == 0

    def project():
        u = _rms(xp_ref[...], vec_ref[R_NORM + 2:R_NORM + 3, :]).astype(bf16)
        u_s[p3] = u
        yield
        for c0 in range(0, D_RNN, sw):
            xrb[p2, SUBLANES:SUBLANES + rb, c0:c0 + sw] = _dot(u, _w(wmain_ref[:, c0:c0 + sw]))
            yield
        for c0 in range(0, D_RNN, nw):
            yr_s[p2, :, c0:c0 + nw] = _dot(u, _w(wmain_ref[:, D_RNN + c0:D_RNN + c0 + nw]))
            yield

    def recur():
        xrb[r2, 0:SUBLANES, :] = jnp.where(r_first, 0.0, tail[...])
        rc = ROW_CHUNK
        sub = lax.broadcasted_iota(jnp.int32, (SUBLANES, sw), 0)
        scan_steps = [(sh, sub >= sh) for sh in (1, 2, 4)]
        first_row = jnp.logical_and(sub == 0, r_first)
        for s in range(D_RNN // sw):
            cs = slice(s * sw, (s + 1) * sw)
            for r0 in range(0, rb, rc):
                ext = xrb[r2, r0:r0 + SUBLANES + rc, cs]
                xc = vec_ref[R_CONVB:R_CONVB + 1, cs]
                for t in range(CONV_W):
                    lag = CONV_W - 1 - t
                    tap = ext if lag == 0 else pltpu.roll(ext, lag, 0)
                    xc = xc + tap[SUBLANES:, :] * vec_ref[R_CONVW + t:R_CONVW + t + 1, cs]
                hbuf[r0:r0 + rc, cs] = xc
            yield
            pre_s[s] = _dot(hbuf[:, cs].astype(bf16), _w(rgbd_ref[s]))
            yield
            hprev = jnp.where(r_first, 0.0, hc[:, cs])
            for r0 in range(0, rb, rc):
                hs = []
                for g0 in range(r0, r0 + rc, SUBLANES):
                    gs = slice(g0, g0 + SUBLANES)
                    xc = hbuf[gs, cs]
                    a, mult, gi = _rglru_coeffs(pre_s[s, gs, 0:sw], pre_s[s, gs, sw:2 * sw], vec_ref, cs)
                    if g0 == 0:
                        mult = jnp.where(first_row, 1.0, mult)
                    b = mult * gi * xc
                    for sh, m in scan_steps:
                        b = jnp.where(m, a * pltpu.roll(b, sh, 0) + b, b)
                        a = jnp.where(m, a * pltpu.roll(a, sh, 0), a)
                    hg = a * hprev + b
                    hs.append(hg)
                    hprev = jnp.broadcast_to(hg[SUBLANES - 1:SUBLANES, :], (SUBLANES, sw))
                h = jnp.concatenate(hs, axis=0)
                ornn_s[r2, r0:r0 + rc, cs] = (h * _gelu_tanh(yr_s[r2, r0:r0 + rc, cs])).astype(bf16)
                if (r0 // rc) % 2 == 1:
                    yield
            hc[:, cs] = hprev
        tail[...] = xrb[r2, rb:rb + SUBLANES, :]

    def attend():
        u = u_s[a3]
        o_rnn = ornn_s[p2]
        proj = []
        for c0 in range(OFF_Q, OFF_OG, nw):
            proj.append(_dot(u, _w(wmain_ref[:, c0:c0 + nw])))
            yield
        q, k = proj[0], proj[1]
        v = jnp.concatenate(proj[2:4], axis=1)
        glog = _decay_log(u, wlr_ref, wlr2_ref, vec_ref)
        yield
        qi, ki, kend, vb, dec, mask = _gla_prepare(q, k, v, glog)
        yield
        og = []
        for c0 in range(OFF_OG, OFF_LR, nw):
            og.append(_dot(u, _w(wmain_ref[:, c0:c0 + nw])))
            yield
        og = jnp.concatenate(og, axis=1)
        gates = []
        o_heads = []
        for h in range(GLA_HEADS):
            oh = _gla_intra(qi, ki, vb, mask, h) + _gla_inter(qi, kend, vb, dec, S, h, a_first)
            o_heads.append((_head_norm(oh, vec_ref, h) * _silu(og[:, h * GLA_DV:(h + 1) * GLA_DV])).astype(bf16))
            yield
            gates.append(_dot(u, _w(wg_ref[:, h * nw:(h + 1) * nw])))
            yield
        part_a = []
        for i, c0 in enumerate(range(0, D_MODEL, nw)):
            part_a.append(_sigmoid(gates[i]) * _dot(o_rnn, _w(wbr_ref[:, c0:c0 + nw])))
            yield
        o_gla = jnp.concatenate(o_heads, axis=1)
        merged = []
        ng = D_MODEL // nw
        for i, c0 in enumerate(range(0, D_MODEL, nw)):
            yb = _dot(o_gla, _w(wbg_ref[:, c0:c0 + nw]))
            merged.append((part_a[i] + _sigmoid(gates[ng + i]) * yb).astype(bf16))
            yield
        merged = jnp.concatenate(merged, axis=1)
        mix = []
        for c0 in range(0, D_MODEL, nw):
            mix.append(_dot(merged, _w(wo_ref[:, c0:c0 + nw])))
            yield
        mix = jnp.concatenate(mix, axis=1)
        y_ref[...] = xa_ref[...] + _rms(mix, vec_ref[R_NORM + 3:R_NORM + 4, :])
        yield

    _interleave(recur(), attend(), project())

    @pl.when((j + bpb - 1) % bpb == bpb - 1)
    def _():
        h_ref[...] = hc[0:1, :]
        conv_ref[...] = tail[SUBLANES - (CONV_W - 1):SUBLANES, :]

    @pl.when((j + 2 * bpb - 2) % bpb == bpb - 1)
    def _():
        s_ref[...] = S[...]


def _mix_prompt_call(x2d, vecs, wmain, wlr, wlr2, wg, rgbd, wbr, wbg, wo, *, bsz, seq):
    rb = GLA_BLOCK
    bpb = seq // rb
    nblk = bsz * bpb
    consts = (vecs, wmain, wlr, wlr2, wg, rgbd, wbr, wbg, wo)

    def blk(d):
        return lambda j: (jnp.clip(j - d, 0, nblk - 1), 0)

    def seq_of(d, nd):
        return lambda j: (jnp.clip(j - d, 0, nblk - 1) // bpb,) + (0,) * nd

    return pl.pallas_call(
        functools.partial(_mix_prompt_kernel, bpb=bpb),
        grid=(nblk + 2,),
        in_specs=[pl.BlockSpec((rb, D_MODEL), blk(0)), pl.BlockSpec((rb, D_MODEL), blk(2))]
        + [_const_spec(w.shape) for w in consts],
        out_specs=[pl.BlockSpec((rb, D_MODEL), blk(2)),
                   pl.BlockSpec((None, 1, D_RNN), seq_of(1, 2)),
                   pl.BlockSpec((None, CONV_W - 1, D_RNN), seq_of(1, 2)),
                   pl.BlockSpec((None, GLA_HEADS, GLA_DK, GLA_DV), seq_of(2, 3))],
        out_shape=[jax.ShapeDtypeStruct((nblk * rb, D_MODEL), f32),
                   jax.ShapeDtypeStruct((bsz, 1, D_RNN), f32),
                   jax.ShapeDtypeStruct((bsz, CONV_W - 1, D_RNN), f32),
                   jax.ShapeDtypeStruct((bsz, GLA_HEADS, GLA_DK, GLA_DV), f32)],
        scratch_shapes=[pltpu.VMEM((3, rb, D_MODEL), bf16),
                        pltpu.VMEM((2, rb + SUBLANES, D_RNN), f32),
                        pltpu.VMEM((2, rb, D_RNN), f32),
                        pltpu.VMEM((2, rb, D_RNN), bf16),
                        pltpu.VMEM((SUBLANES, D_RNN), f32),
                        pltpu.VMEM((SUBLANES, D_RNN), f32),
                        pltpu.VMEM((GLA_HEADS, GLA_DK, GLA_DV), f32),
                        pltpu.VMEM((rb, D_RNN), f32),
                        pltpu.VMEM((D_RNN // (2 * RNN_BW), rb, 4 * RNN_BW), f32)],
        compiler_params=pltpu.CompilerParams(dimension_semantics=("arbitrary",), vmem_limit_bytes=VMEM_LIMIT),
        name="mix_prompt",
    )(x2d, x2d, *consts)


def _sample_pre_kernel(x_ref, h0_ref, c0_ref, vec_ref, wgu_ref, wd_ref, wmain_ref, wlr_ref, wlr2_ref, wg_ref, rgbd_ref,
                       x1_ref, hn_ref, cn_ref, ornn_ref, qkvg_ref, glog_ref, gates_ref):
    x1 = _ffn(x_ref[...], vec_ref[R_NORM:R_NORM + 1, :], vec_ref[R_NORM + 1:R_NORM + 2, :], wgu_ref, wd_ref)
    x1_ref[...] = x1
    u = _rms(x1, vec_ref[R_NORM + 2:R_NORM + 3, :]).astype(bf16)
    xy = _dot(u, _w(wmain_ref[:, 0:2 * D_RNN]))
    xr = xy[:, :D_RNN]
    yr = xy[:, D_RNN:]
    xc = vec_ref[R_CONVB:R_CONVB + 1, :]
    for j in range(CONV_W - 1):
        xc = xc + c0_ref[:, j * D_RNN:(j + 1) * D_RNN] * vec_ref[R_CONVW + j:R_CONVW + j + 1, :]
    xc = xc + xr * vec_ref[R_CONVW + CONV_W - 1:R_CONVW + CONV_W, :]
    cn_ref[:, 0:(CONV_W - 2) * D_RNN] = c0_ref[:, D_RNN:(CONV_W - 1) * D_RNN]
    cn_ref[:, (CONV_W - 2) * D_RNN:] = xr
    ra, ri = _rglru_preact(xc, rgbd_ref)
    a, mult, gi = _rglru_coeffs(ra, ri, vec_ref)
    h = a * h0_ref[...] + mult * gi * xc
    hn_ref[...] = h
    ornn_ref[...] = (h * _gelu_tanh(yr)).astype(bf16)
    qkvg_ref[...] = _dot(u, _w(wmain_ref[:, OFF_Q:OFF_LR]))
    glog_ref[...] = _decay_log(u, wlr_ref, wlr2_ref, vec_ref)
    gates_ref[...] = _dot(u, _w(wg_ref[...]))


def _sample_pre_call(xs, h0, c0, vecs, wgu, wd, wmain, wlr, wlr2, wg, rgbd):
    n = xs.shape[0]
    args = (xs, h0, c0, vecs, wgu, wd, wmain, wlr, wlr2, wg, rgbd)
    outs = [((n, D_MODEL), f32), ((n, D_RNN), f32), ((n, (CONV_W - 1) * D_RNN), f32), ((n, D_RNN), bf16),
            ((n, OFF_LR - OFF_Q), f32), ((n, D_QK), f32), ((n, 2 * D_MODEL), f32)]
    return pl.pallas_call(
        _sample_pre_kernel,
        grid=(1,),
        in_specs=[_const_spec(a.shape) for a in args],
        out_specs=[pl.BlockSpec(s, lambda i: (0, 0)) for s, _ in outs],
        out_shape=[jax.ShapeDtypeStruct(s, d) for s, d in outs],
        compiler_params=pltpu.CompilerParams(dimension_semantics=("arbitrary",), vmem_limit_bytes=VMEM_LIMIT),
        name="sample_pre",
    )(*args)


def _col_bcast(rows8, width):
    tiled = jnp.concatenate([rows8] * (LANES // SUBLANES), axis=0)
    tt = jnp.transpose(tiled)
    return [jnp.broadcast_to(tt[:, j:j + 1], (LANES, width)) for j in range(SUBLANES)]


def _sample_gla_kernel(q_ref, k_ref, v_ref, g_ref, s0_ref, sn_ref, o_ref):
    scale = GLA_DK ** -0.5
    for h in range(GLA_HEADS):
        ks = slice(h * GLA_DK, (h + 1) * GLA_DK)
        vs = slice(h * GLA_DV, (h + 1) * GLA_DV)
        dcols = _col_bcast(jnp.exp(g_ref[:, ks]), GLA_DV)
        kcols = _col_bcast(k_ref[:, ks], GLA_DV)
        qcols = _col_bcast(q_ref[:, ks] * scale, GLA_DV)
        for j in range(SUBLANES):
            vrow = v_ref[j:j + 1, vs]
            sn = dcols[j] * s0_ref[j, h] + kcols[j] * vrow
            sn_ref[j, h] = sn
            o_ref[j:j + 1, vs] = jnp.sum(qcols[j] * sn, axis=0, keepdims=True)


def _sample_gla_call(q, k, v, g, s0):
    n = q.shape[0]
    sb = (SUBLANES, GLA_HEADS, GLA_DK, GLA_DV)
    return pl.pallas_call(
        _sample_gla_kernel,
        grid=(n // SUBLANES,),
        in_specs=[pl.BlockSpec((SUBLANES, D_QK), lambda i: (i, 0)),
                  pl.BlockSpec((SUBLANES, D_QK), lambda i: (i, 0)),
                  pl.BlockSpec((SUBLANES, D_V), lambda i: (i, 0)),
                  pl.BlockSpec((SUBLANES, D_QK), lambda i: (i, 0)),
                  pl.BlockSpec(sb, lambda i: (i, 0, 0, 0))],
        out_specs=[pl.BlockSpec(sb, lambda i: (i, 0, 0, 0)),
                   pl.BlockSpec((SUBLANES, D_V), lambda i: (i, 0))],
        out_shape=[jax.ShapeDtypeStruct(s0.shape, f32), jax.ShapeDtypeStruct((n, D_V), f32)],
        compiler_params=pltpu.CompilerParams(dimension_semantics=("arbitrary",), vmem_limit_bytes=VMEM_LIMIT),
        name="sample_gla",
    )(q, k, v, g, s0)


def _sample_post_kernel(x1_ref, ornn_ref, o_ref, og_ref, gates_ref, vec_ref, wbr_ref, wbg_ref, wo_ref, wgu_ref, wd_ref,
                        y_ref):
    o = o_ref[...]
    og = og_ref[...]
    o_gla = jnp.concatenate(
        [_head_norm(o[:, h * GLA_DV:(h + 1) * GLA_DV], vec_ref, h) * _silu(og[:, h * GLA_DV:(h + 1) * GLA_DV])
         for h in range(GLA_HEADS)], axis=1).astype(bf16)
    gates = gates_ref[...]
    ya = _dot(ornn_ref[...], _w(wbr_ref[...]))
    yb = _dot(o_gla, _w(wbg_ref[...]))
    merged = _sigmoid(gates[:, :D_MODEL]) * ya + _sigmoid(gates[:, D_MODEL:]) * yb
    mix = _dot(merged.astype(bf16), _w(wo_ref[...]))
    x2 = x1_ref[...] + _rms(mix, vec_ref[R_NORM + 3:R_NORM + 4, :])
    y_ref[...] = _ffn(x2, vec_ref[R_NORM + 4:R_NORM + 5, :], vec_ref[R_NORM + 5:R_NORM + 6, :], wgu_ref, wd_ref)


def _sample_post_call(x1, ornn, o, og, gates, vecs, wbr, wbg, wo, wgu, wd):
    n = x1.shape[0]
    args = (x1, ornn, o, og, gates, vecs, wbr, wbg, wo, wgu, wd)
    return pl.pallas_call(
        _sample_post_kernel,
        grid=(1,),
        in_specs=[_const_spec(a.shape) for a in args],
        out_specs=pl.BlockSpec((n, D_MODEL), lambda i: (0, 0)),
        out_shape=jax.ShapeDtypeStruct((n, D_MODEL), f32),
        compiler_params=pltpu.CompilerParams(dimension_semantics=("arbitrary",), vmem_limit_bytes=VMEM_LIMIT),
        name="sample_post",
    )(*args)


def _pack_kernel(w_ref, o_ref):
    o_ref[...] = pltpu.bitcast(w_ref[...].astype(bf16), jnp.uint32)


def _pack(w, *, col0=0, ncols=None, name):
    nl, k, n = w.shape
    ncols = n if ncols is None else ncols
    bn = PACK_COLS if ncols % PACK_COLS == 0 else ncols
    assert col0 % bn == 0 and ncols % bn == 0
    return pl.pallas_call(
        _pack_kernel,
        grid=(nl, ncols // bn),
        in_specs=[pl.BlockSpec((None, k, bn), lambda l, j: (l, 0, j + col0 // bn))],
        out_specs=pl.BlockSpec((None, k // 2, bn), lambda l, j: (l, 0, j)),
        out_shape=jax.ShapeDtypeStruct((nl, k // 2, ncols), jnp.uint32),
        compiler_params=pltpu.CompilerParams(dimension_semantics=("arbitrary", "arbitrary"),
                                             vmem_limit_bytes=VMEM_LIMIT),
        name=name,
    )(w)


def _block_diag_gates(w_a, w_x):
    def bd(w):
        w = w.reshape(RNN_BLOCKS // 2, 2, RNN_BW, RNN_BW)
        z = jnp.zeros_like(w[:, 0])
        top = jnp.concatenate([w[:, 0], z], axis=2)
        bot = jnp.concatenate([z, w[:, 1]], axis=2)
        return jnp.concatenate([top, bot], axis=1)
    return jnp.concatenate([bd(w_a), bd(w_x)], axis=2)


def kernel(x_prompt, x_sample, state_rnn_h, state_rnn_conv, state_gla, norm_gains, ffn1_w_gu, ffn1_w_down, w_in, conv_w, conv_b, rg_w_a, rg_b_a, rg_w_x, rg_b_x, rg_lambda, gla_w_lr, gla_b_lr, gla_norm_g, w_branch_rnn, w_branch_gla, w_out, ffn2_w_gu, ffn2_w_down):
    assert w_in.shape == (1, D_MODEL, D_IN) and x_sample.shape[1] == 1
    bsz, seq, _ = x_prompt.shape
    nsmp = x_sample.shape[0]

    vecs = jnp.concatenate([
        norm_gains[0], conv_w[0], conv_b, rg_b_a, rg_b_x, rg_lambda,
        jnp.pad(gla_b_lr, ((0, 0), (0, D_MODEL - D_QK))), jnp.tile(gla_norm_g, (1, GLA_HEADS))], axis=0).astype(f32)
    w1gu = _pack(ffn1_w_gu, name="pack_w1gu")[0]
    w1d = _pack(ffn1_w_down, name="pack_w1d")[0]
    w2gu = _pack(ffn2_w_gu, name="pack_w2gu")[0]
    w2d = _pack(ffn2_w_down, name="pack_w2d")[0]
    wmain = _pack(w_in, ncols=OFF_LR, name="pack_wmain")[0]
    wlr = _pack(jnp.pad(w_in[:, :, OFF_LR:OFF_GATES], ((0, 0), (0, 0), (0, LANES - GLA_RANK))), name="pack_wlr")[0]
    wlr2 = _pack(jnp.pad(gla_w_lr, ((0, 0), (0, LANES - GLA_RANK), (0, 0))), name="pack_wlr2")[0]
    wg = _pack(w_in[:, :, OFF_GATES:], name="pack_wg")[0]
    rgbd = _pack(_block_diag_gates(rg_w_a[0], rg_w_x[0]), name="pack_rgbd")
    wbr = _pack(w_branch_rnn, name="pack_wbr")[0]
    wbg = _pack(w_branch_gla, name="pack_wbg")[0]
    wo = _pack(w_out, name="pack_wo")[0]

    xp = x_prompt.reshape(bsz * seq, D_MODEL)
    x1 = _ffn_call(xp, vecs, w1gu, w1d, row_pre=R_NORM, tm=512, name="ffn1_prompt")
    x2, hp, cp, sp = _mix_prompt_call(x1, vecs, wmain, wlr, wlr2, wg, rgbd, wbr, wbg, wo, bsz=bsz, seq=seq)
    yp = _ffn_call(x2, vecs, w2gu, w2d, row_pre=R_NORM + 4, tm=512, name="ffn2_prompt")

    xs = x_sample.reshape(nsmp, D_MODEL)
    c0 = state_rnn_conv[0].reshape(nsmp, (CONV_W - 1) * D_RNN)
    x1s, hs, cs, ornn, qkvg, glog, gates = _sample_pre_call(
        xs, state_rnn_h[0], c0, vecs, w1gu, w1d, wmain, wlr, wlr2, wg, rgbd)
    ss, osmp = _sample_gla_call(qkvg[:, 0:D_QK], qkvg[:, D_QK:2 * D_QK], qkvg[:, 2 * D_QK:2 * D_QK + D_V], glog,
                                state_gla[0])
    ys = _sample_post_call(x1s, ornn, osmp, qkvg[:, 2 * D_QK + D_V:], gates, vecs, wbr, wbg, wo, w2gu, w2d)

    return (yp.reshape(bsz, seq, D_MODEL), ys.reshape(nsmp, 1, D_MODEL),
            hp.reshape(1, bsz, D_RNN), cp[None], sp[None],
            hs[None], cs.reshape(1, nsmp, CONV_W - 1, D_RNN), ss[None])
```

```python
import functools
import math

import jax
import jax.numpy as jnp
from jax import lax
from jax.experimental import pallas as pl
from jax.experimental.pallas import tpu as pltpu

D_MODEL = 1024
D_RNN = 1024
RNN_BLOCKS = 8
RNN_BW = D_RNN // RNN_BLOCKS
CONV_W = 4
RG_C = 8.0
GLA_HEADS = 4
GLA_DK = 128
GLA_DV = 256
GLA_RANK = 16
GLA_TAU = 16.0
GLA_CHUNK = 64
D_FF = 2816
EPS = 1e-6

D_QK = GLA_HEADS * GLA_DK
D_V = GLA_HEADS * GLA_DV
OFF_Q = 2 * D_RNN
OFF_K = OFF_Q + D_QK
OFF_V = OFF_K + D_QK
OFF_OG = OFF_V + D_V
OFF_LR = OFF_OG + D_V
OFF_GATES = OFF_LR + GLA_RANK
D_IN = OFF_GATES + 2 * D_MODEL

LANES = 128
SUBLANES = 8
VMEM_LIMIT = 56 * 1024 * 1024
PACK_COLS = 512
ROW_CHUNK = 32
GLA_BLOCK = 256

R_NORM = 0
R_CONVW = 6
R_CONVB = 10
R_BA = 11
R_BX = 12
R_LAM = 13
R_BLR = 14
R_GNORM = 15

bf16 = jnp.bfloat16
f32 = jnp.float32


def _w(packed):
    return pltpu.bitcast(packed, bf16)


def _dot(a, b):
    return jnp.dot(a, b, preferred_element_type=f32)


def _rms(x, g):
    return x * lax.rsqrt(jnp.mean(x * x, axis=-1, keepdims=True) + EPS) * g


def _sigmoid(x):
    return 0.5 * jnp.tanh(0.5 * x) + 0.5


def _silu(x):
    hx = 0.5 * x
    return hx * jnp.tanh(hx) + hx


def _softplus(x):
    return jnp.maximum(x, 0.0) + jnp.log1p(jnp.exp(-jnp.abs(x)))


def _sqrt_nonneg(m):
    return jnp.where(m > 0.0, m * lax.rsqrt(m), 0.0)


def _gelu_tanh(x):
    c = math.sqrt(2.0 / math.pi)
    hx = 0.5 * x
    return hx + hx * jnp.tanh(x * (c + (0.044715 * c) * (x * x)))


def _ffn(x, g_pre, g_post, wgu_ref, wd_ref):
    u = _rms(x, g_pre).astype(bf16)
    gu = _dot(u, _w(wgu_ref[...]))
    h = (_silu(gu[:, :D_FF]) * gu[:, D_FF:]).astype(bf16)
    y = _dot(h, _w(wd_ref[...]))
    return x + 0.5 * _rms(y, g_post)


def _rglru_preact(xc, rgbd_ref):
    xcb = xc.astype(bf16)
    slab = 2 * RNN_BW
    ra, ri = [], []
    for s in range(D_RNN // slab):
        gi = _dot(xcb[:, s * slab:(s + 1) * slab], _w(rgbd_ref[s]))
        ra.append(gi[:, :slab])
        ri.append(gi[:, slab:])
    return jnp.concatenate(ra, axis=1), jnp.concatenate(ri, axis=1)


def _rglru_coeffs(ra, ri, vec_ref, cols=slice(None)):
    r = _sigmoid(ra + vec_ref[R_BA:R_BA + 1, cols])
    i = _sigmoid(ri + vec_ref[R_BX:R_BX + 1, cols])
    log_a = -RG_C * r * _softplus(-vec_ref[R_LAM:R_LAM + 1, cols])
    a = jnp.exp(log_a)
    mult = _sqrt_nonneg(jnp.maximum(-(jnp.tanh(log_a) * (a * a + 1.0)), 0.0))
    return a, mult, i


def _head_norm(oh, vec_ref, h):
    gh = vec_ref[R_GNORM:R_GNORM + 1, h * GLA_DV:(h + 1) * GLA_DV]
    return oh * lax.rsqrt(jnp.mean(oh * oh, axis=-1, keepdims=True) + EPS) * gh


def _decay_log(u, wlr_ref, wlr2_ref, vec_ref):
    lr = _dot(u, _w(wlr_ref[...]))
    z = _dot(lr.astype(bf16), _w(wlr2_ref[...])) + vec_ref[R_BLR:R_BLR + 1, 0:D_QK]
    softplus_neg = jnp.maximum(-z, 0.0) + jnp.log(1.0 + jnp.exp(-jnp.abs(z)))
    return -softplus_neg * (1.0 / GLA_TAU)


def _ffn_kernel(x_ref, vec_ref, wgu_ref, wd_ref, o_ref, *, row_pre):
    o_ref[...] = _ffn(x_ref[...], vec_ref[row_pre:row_pre + 1, :], vec_ref[row_pre + 1:row_pre + 2, :],
                      wgu_ref, wd_ref)


def _const_spec(shape):
    nd = len(shape)
    return pl.BlockSpec(shape, lambda *_: (0,) * nd, pipeline_mode=pl.Buffered(1))


def _ffn_call(x2d, vecs, wgu, wd, *, row_pre, tm, name):
    n = x2d.shape[0]
    return pl.pallas_call(
        functools.partial(_ffn_kernel, row_pre=row_pre),
        grid=(n // tm,),
        in_specs=[pl.BlockSpec((tm, D_MODEL), lambda i: (i, 0)),
                  _const_spec(vecs.shape), _const_spec(wgu.shape), _const_spec(wd.shape)],
        out_specs=pl.BlockSpec((tm, D_MODEL), lambda i: (i, 0)),
        out_shape=jax.ShapeDtypeStruct((n, D_MODEL), f32),
        compiler_params=pltpu.CompilerParams(dimension_semantics=("arbitrary",), vmem_limit_bytes=VMEM_LIMIT),
        name=name,
    )(x2d, vecs, wgu, wd)


def _gla_prepare(q, k, v, glog):
    rows = q.shape[0]
    c = GLA_CHUNK
    shift = c.bit_length() - 1
    ri = lax.broadcasted_iota(jnp.int32, (rows, rows), 0)
    ci = lax.broadcasted_iota(jnp.int32, (rows, rows), 1)
    mask = jnp.logical_and(lax.shift_right_logical(ri, shift) == lax.shift_right_logical(ci, shift), ri >= ci)
    ltri = mask.astype(bf16)
    g1 = glog.astype(bf16)
    r1 = glog - g1.astype(f32)
    g2 = r1.astype(bf16)
    g3 = (r1 - g2.astype(f32)).astype(bf16)
    bcum = _dot(ltri, g1) + _dot(ltri, g2) + _dot(ltri, g3)
    blast = [bcum[(n + 1) * c - 1:(n + 1) * c, :] for n in range(rows // c)]
    blast_b = jnp.concatenate([jnp.broadcast_to(b, (c, D_QK)) for b in blast], axis=0)
    qi = (q * (GLA_DK ** -0.5) * jnp.exp(bcum)).astype(bf16)
    ki = (k * jnp.exp(-bcum)).astype(bf16)
    kend = (k * jnp.exp(blast_b - bcum)).astype(bf16)
    dec = [jnp.exp(b) for b in blast]
    return qi, ki, kend, v.astype(bf16), dec, mask


def _gla_intra(qi, ki, vb, mask, h):
    ks = slice(h * GLA_DK, (h + 1) * GLA_DK)
    vs = slice(h * GLA_DV, (h + 1) * GLA_DV)
    att = lax.dot_general(qi[:, ks], ki[:, ks], (((1,), (1,)), ((), ())), preferred_element_type=f32)
    att = jnp.where(mask, att, 0.0).astype(bf16)
    return _dot(att, vb[:, vs])


def _gla_inter(qi, kend, vb, dec, S, h, reset):
    c = GLA_CHUNK
    ks = slice(h * GLA_DK, (h + 1) * GLA_DK)
    vs = slice(h * GLA_DV, (h + 1) * GLA_DV)
    nch = qi.shape[0] // c
    upd, dcol = [], []
    for n in range(nch):
        rs = slice(n * c, (n + 1) * c)
        upd.append(lax.dot_general(kend[rs, ks], vb[rs, vs], (((0,), (0,)), ((), ())), preferred_element_type=f32))
        d = jnp.transpose(jnp.broadcast_to(dec[n][:, ks], (GLA_DK, GLA_DK)))
        dcol.append(jnp.concatenate([d, d], axis=1))
    sh = jnp.where(reset, 0.0, S[h])
    outs = []
    for n in range(nch):
        rs = slice(n * c, (n + 1) * c)
        outs.append(_dot(qi[rs, ks], sh.astype(bf16)))
        sh = sh * dcol[n] + upd[n]
    S[h] = sh
    return jnp.concatenate(outs, axis=0)


def _interleave(order, **gens):
    def step(k):
        if k in gens:
            try:
                next(gens[k])
            except StopIteration:
                del gens[k]
    for k in order:
        step(k)
    while gens:
        for k in list(gens):
            step(k)


def _mix_prompt_kernel(xp_ref, xa_ref, vec_ref, wmain_ref, wlr_ref, wlr2_ref, wg_ref, rgbd_ref, wbr_ref, wbg_ref,
                       wo_ref, y_ref, h_ref, conv_ref, s_ref,
                       u_s, xrb, yr_s, ornn_s, hc, tail, S, hbuf, pre_s, *, bpb):
    j = pl.program_id(0)
    rb = GLA_BLOCK
    sw = 2 * RNN_BW
    nw = 512

    @pl.when(j == 0)
    def _():
        u_s[...] = jnp.zeros_like(u_s)
        xrb[...] = jnp.zeros_like(xrb)
        yr_s[...] = jnp.zeros_like(yr_s)
        ornn_s[...] = jnp.zeros_like(ornn_s)
        hc[...] = jnp.zeros_like(hc)
        tail[...] = jnp.zeros_like(tail)
        S[...] = jnp.zeros_like(S)

    p3 = j % 3
    p2 = j % 2
    r2 = (j + 1) % 2
    a3 = (j + 1) % 3
    r_first = (j + bpb - 1) % bpb == 0
    a_first = (j + 2 * bpb - 2) % bpb == 0

    def project():
        u = _rms(xp_ref[...], vec_ref[R_NORM + 2:R_NORM + 3, :]).astype(bf16)
        u_s[p3] = u
        yield
        for c0 in range(0, D_RNN, sw):
            xrb[p2, SUBLANES:SUBLANES + rb, c0:c0 + sw] = _dot(u, _w(wmain_ref[:, c0:c0 + sw]))
            yield
        for c0 in range(0, D_RNN, nw):
            yr_s[p2, :, c0:c0 + nw] = _dot(u, _w(wmain_ref[:, D_RNN + c0:D_RNN + c0 + nw]))
            yield

    def recur():
        xrb[r2, 0:SUBLANES, :] = jnp.where(r_first, 0.0, tail[...])
        rc = ROW_CHUNK
        sub = lax.broadcasted_iota(jnp.int32, (SUBLANES, sw), 0)
        scan_steps = [(sh, sub >= sh) for sh in (1, 2, 4)]
        first_row = jnp.logical_and(sub == 0, r_first)
        for s in range(D_RNN // sw):
            cs = slice(s * sw, (s + 1) * sw)
            for r0 in range(0, rb, rc):
                ext = xrb[r2, r0:r0 + SUBLANES + rc, cs]
                xc = vec_ref[R_CONVB:R_CONVB + 1, cs]
                for t in range(CONV_W):
                    lag = CONV_W - 1 - t
                    tap = ext if lag == 0 else pltpu.roll(ext, lag, 0)
                    xc = xc + tap[SUBLANES:, :] * vec_ref[R_CONVW + t:R_CONVW + t + 1, cs]
                hbuf[r0:r0 + rc, cs] = xc
            yield
            pre_s[s] = _dot(hbuf[:, cs].astype(bf16), _w(rgbd_ref[s]))
            yield
            hprev = jnp.where(r_first, 0.0, hc[:, cs])
            for r0 in range(0, rb, rc):
                hs = []
                for g0 in range(r0, r0 + rc, SUBLANES):
                    gs = slice(g0, g0 + SUBLANES)
                    xc = hbuf[gs, cs]
                    a, mult, gi = _rglru_coeffs(pre_s[s, gs, 0:sw], pre_s[s, gs, sw:2 * sw], vec_ref, cs)
                    if g0 == 0:
                        mult = jnp.where(first_row, 1.0, mult)
                    b = mult * gi * xc
                    for sh, m in scan_steps:
                        b = jnp.where(m, a * pltpu.roll(b, sh, 0) + b, b)
                        a = jnp.where(m, a * pltpu.roll(a, sh, 0), a)
                    hg = a * hprev + b
                    hs.append(hg)
                    hprev = jnp.broadcast_to(hg[SUBLANES - 1:SUBLANES, :], (SUBLANES, sw))
                h = jnp.concatenate(hs, axis=0)
                ornn_s[r2, r0:r0 + rc, cs] = (h * _gelu_tanh(yr_s[r2, r0:r0 + rc, cs])).astype(bf16)
                if (r0 // rc) % 2 == 1:
                    yield
            hc[:, cs] = hprev
        tail[...] = xrb[r2, rb:rb + SUBLANES, :]

    def attend():
        u = u_s[a3]
        o_rnn = ornn_s[p2]
        proj = []
        for c0 in range(OFF_Q, OFF_OG, nw):
            proj.append(_dot(u, _w(wmain_ref[:, c0:c0 + nw])))
            yield
        q, k = proj[0], proj[1]
        v = jnp.concatenate(proj[2:4], axis=1)
        glog = _decay_log(u, wlr_ref, wlr2_ref, vec_ref)
        yield
        qi, ki, kend, vb, dec, mask = _gla_prepare(q, k, v, glog)
        yield
        og = []
        for c0 in range(OFF_OG, OFF_LR, nw):
            og.append(_dot(u, _w(wmain_ref[:, c0:c0 + nw])))
            yield
        og = jnp.concatenate(og, axis=1)
        gates = []
        o_heads = []
        for h in range(GLA_HEADS):
            oh = _gla_intra(qi, ki, vb, mask, h) + _gla_inter(qi, kend, vb, dec, S, h, a_first)
            o_heads.append((_head_norm(oh, vec_ref, h) * _silu(og[:, h * GLA_DV:(h + 1) * GLA_DV])).astype(bf16))
            yield
            gates.append(_dot(u, _w(wg_ref[:, h * nw:(h + 1) * nw])))
            yield
        part_a = []
        for i, c0 in enumerate(range(0, D_MODEL, nw)):
            part_a.append(_sigmoid(gates[i]) * _dot(o_rnn, _w(wbr_ref[:, c0:c0 + nw])))
            yield
        o_gla = jnp.concatenate(o_heads, axis=1)
        merged = []
        ng = D_MODEL // nw
        for i, c0 in enumerate(range(0, D_MODEL, nw)):
            yb = _dot(o_gla, _w(wbg_ref[:, c0:c0 + nw]))
            merged.append((part_a[i] + _sigmoid(gates[ng + i]) * yb).astype(bf16))
            yield
        merged = jnp.concatenate(merged, axis=1)
        mix = []
        for c0 in range(0, D_MODEL, nw):
            mix.append(_dot(merged, _w(wo_ref[:, c0:c0 + nw])))
            yield
        mix = jnp.concatenate(mix, axis=1)
        y_ref[...] = xa_ref[...] + _rms(mix, vec_ref[R_NORM + 3:R_NORM + 4, :])
        yield

    _interleave("", A=attend(), R=recur(), P=project())

    @pl.when((j + bpb - 1) % bpb == bpb - 1)
    def _():
        h_ref[...] = hc[0:1, :]
        conv_ref[...] = tail[SUBLANES - (CONV_W - 1):SUBLANES, :]

    @pl.when((j + 2 * bpb - 2) % bpb == bpb - 1)
    def _():
        s_ref[...] = S[...]


def _mix_prompt_call(x2d, vecs, wmain, wlr, wlr2, wg, rgbd, wbr, wbg, wo, *, bsz, seq):
    rb = GLA_BLOCK
    bpb = seq // rb
    nblk = bsz * bpb
    consts = (vecs, wmain, wlr, wlr2, wg, rgbd, wbr, wbg, wo)

    def blk(d):
        return lambda j: (jnp.clip(j - d, 0, nblk - 1), 0)

    def seq_of(d, nd):
        return lambda j: (jnp.clip(j - d, 0, nblk - 1) // bpb,) + (0,) * nd

    return pl.pallas_call(
        functools.partial(_mix_prompt_kernel, bpb=bpb),
        grid=(nblk + 2,),
        in_specs=[pl.BlockSpec((rb, D_MODEL), blk(0)), pl.BlockSpec((rb, D_MODEL), blk(2))]
        + [_const_spec(w.shape) for w in consts],
        out_specs=[pl.BlockSpec((rb, D_MODEL), blk(2)),
                   pl.BlockSpec((None, 1, D_RNN), seq_of(1, 2)),
                   pl.BlockSpec((None, CONV_W - 1, D_RNN), seq_of(1, 2)),
                   pl.BlockSpec((None, GLA_HEADS, GLA_DK, GLA_DV), seq_of(2, 3))],
        out_shape=[jax.ShapeDtypeStruct((nblk * rb, D_MODEL), f32),
                   jax.ShapeDtypeStruct((bsz, 1, D_RNN), f32),
                   jax.ShapeDtypeStruct((bsz, CONV_W - 1, D_RNN), f32),
                   jax.ShapeDtypeStruct((bsz, GLA_HEADS, GLA_DK, GLA_DV), f32)],
        scratch_shapes=[pltpu.VMEM((3, rb, D_MODEL), bf16),
                        pltpu.VMEM((2, rb + SUBLANES, D_RNN), f32),
                        pltpu.VMEM((2, rb, D_RNN), f32),
                        pltpu.VMEM((2, rb, D_RNN), bf16),
                        pltpu.VMEM((SUBLANES, D_RNN), f32),
                        pltpu.VMEM((SUBLANES, D_RNN), f32),
                        pltpu.VMEM((GLA_HEADS, GLA_DK, GLA_DV), f32),
                        pltpu.VMEM((rb, D_RNN), f32),
                        pltpu.VMEM((D_RNN // (2 * RNN_BW), rb, 4 * RNN_BW), f32)],
        compiler_params=pltpu.CompilerParams(dimension_semantics=("arbitrary",), vmem_limit_bytes=VMEM_LIMIT),
        name="mix_prompt",
    )(x2d, x2d, *consts)


def _sample_pre_kernel(x_ref, h0_ref, c0_ref, vec_ref, wgu_ref, wd_ref, wmain_ref, wlr_ref, wlr2_ref, wg_ref, rgbd_ref,
                       x1_ref, hn_ref, cn_ref, ornn_ref, qkvg_ref, glog_ref, gates_ref):
    x1 = _ffn(x_ref[...], vec_ref[R_NORM:R_NORM + 1, :], vec_ref[R_NORM + 1:R_NORM + 2, :], wgu_ref, wd_ref)
    x1_ref[...] = x1
    u = _rms(x1, vec_ref[R_NORM + 2:R_NORM + 3, :]).astype(bf16)
    xy = _dot(u, _w(wmain_ref[:, 0:2 * D_RNN]))
    xr = xy[:, :D_RNN]
    yr = xy[:, D_RNN:]
    xc = vec_ref[R_CONVB:R_CONVB + 1, :]
    for j in range(CONV_W - 1):
        xc = xc + c0_ref[j] * vec_ref[R_CONVW + j:R_CONVW + j + 1, :]
    xc = xc + xr * vec_ref[R_CONVW + CONV_W - 1:R_CONVW + CONV_W, :]
    for j in range(CONV_W - 2):
        cn_ref[j] = c0_ref[j + 1]
    cn_ref[CONV_W - 2] = xr
    ra, ri = _rglru_preact(xc, rgbd_ref)
    a, mult, gi = _rglru_coeffs(ra, ri, vec_ref)
    h = a * h0_ref[...] + mult * gi * xc
    hn_ref[...] = h
    ornn_ref[...] = (h * _gelu_tanh(yr)).astype(bf16)
    qkvg_ref[...] = _dot(u, _w(wmain_ref[:, OFF_Q:OFF_LR]))
    glog_ref[...] = _decay_log(u, wlr_ref, wlr2_ref, vec_ref)
    gates_ref[...] = _dot(u, _w(wg_ref[...]))


def _sample_pre_call(xs, h0, c0, vecs, wgu, wd, wmain, wlr, wlr2, wg, rgbd):
    n = xs.shape[0]
    args = (xs, h0, c0, vecs, wgu, wd, wmain, wlr, wlr2, wg, rgbd)
    outs = [((n, D_MODEL), f32), ((n, D_RNN), f32), ((CONV_W - 1, n, D_RNN), f32), ((n, D_RNN), bf16),
            ((n, OFF_LR - OFF_Q), f32), ((n, D_QK), f32), ((n, 2 * D_MODEL), f32)]
    return pl.pallas_call(
        _sample_pre_kernel,
        grid=(1,),
        in_specs=[_const_spec(a.shape) for a in args],
        out_specs=[pl.BlockSpec(s, lambda i, nd=len(s): (0,) * nd) for s, _ in outs],
        out_shape=[jax.ShapeDtypeStruct(s, d) for s, d in outs],
        compiler_params=pltpu.CompilerParams(dimension_semantics=("arbitrary",), vmem_limit_bytes=VMEM_LIMIT),
        name="sample_pre",
    )(*args)


def _col_bcast(rows8, width):
    tiled = jnp.concatenate([rows8] * (LANES // SUBLANES), axis=0)
    tt = jnp.transpose(tiled)
    return [jnp.broadcast_to(tt[:, j:j + 1], (LANES, width)) for j in range(SUBLANES)]


def _sample_gla_kernel(q_ref, k_ref, v_ref, g_ref, s0_ref, sn_ref, o_ref):
    scale = GLA_DK ** -0.5
    for h in range(GLA_HEADS):
        ks = slice(h * GLA_DK, (h + 1) * GLA_DK)
        vs = slice(h * GLA_DV, (h + 1) * GLA_DV)
        dcols = _col_bcast(jnp.exp(g_ref[:, ks]), GLA_DV)
        kcols = _col_bcast(k_ref[:, ks], GLA_DV)
        qcols = _col_bcast(q_ref[:, ks] * scale, GLA_DV)
        for j in range(SUBLANES):
            vrow = v_ref[j:j + 1, vs]
            sn = dcols[j] * s0_ref[j, h] + kcols[j] * vrow
            sn_ref[j, h] = sn
            o_ref[j:j + 1, vs] = jnp.sum(qcols[j] * sn, axis=0, keepdims=True)


def _sample_gla_call(q, k, v, g, s0):
    n = q.shape[0]
    sb = (SUBLANES, GLA_HEADS, GLA_DK, GLA_DV)
    return pl.pallas_call(
        _sample_gla_kernel,
        grid=(n // SUBLANES,),
        in_specs=[pl.BlockSpec((SUBLANES, D_QK), lambda i: (i, 0)),
                  pl.BlockSpec((SUBLANES, D_QK), lambda i: (i, 0)),
                  pl.BlockSpec((SUBLANES, D_V), lambda i: (i, 0)),
                  pl.BlockSpec((SUBLANES, D_QK), lambda i: (i, 0)),
                  pl.BlockSpec(sb, lambda i: (i, 0, 0, 0))],
        out_specs=[pl.BlockSpec(sb, lambda i: (i, 0, 0, 0)),
                   pl.BlockSpec((SUBLANES, D_V), lambda i: (i, 0))],
        out_shape=[jax.ShapeDtypeStruct(s0.shape, f32), jax.ShapeDtypeStruct((n, D_V), f32)],
        compiler_params=pltpu.CompilerParams(dimension_semantics=("arbitrary",), vmem_limit_bytes=VMEM_LIMIT),
        name="sample_gla",
    )(q, k, v, g, s0)


def _sample_post_kernel(x1_ref, ornn_ref, o_ref, og_ref, gates_ref, vec_ref, wbr_ref, wbg_ref, wo_ref, wgu_ref, wd_ref,
                        y_ref):
    o = o_ref[...]
    og = og_ref[...]
    o_gla = jnp.concatenate(
        [_head_norm(o[:, h * GLA_DV:(h + 1) * GLA_DV], vec_ref, h) * _silu(og[:, h * GLA_DV:(h + 1) * GLA_DV])
         for h in range(GLA_HEADS)], axis=1).astype(bf16)
    gates = gates_ref[...]
    ya = _dot(ornn_ref[...], _w(wbr_ref[...]))
    yb = _dot(o_gla, _w(wbg_ref[...]))
    merged = _sigmoid(gates[:, :D_MODEL]) * ya + _sigmoid(gates[:, D_MODEL:]) * yb
    mix = _dot(merged.astype(bf16), _w(wo_ref[...]))
    x2 = x1_ref[...] + _rms(mix, vec_ref[R_NORM + 3:R_NORM + 4, :])
    y_ref[...] = _ffn(x2, vec_ref[R_NORM + 4:R_NORM + 5, :], vec_ref[R_NORM + 5:R_NORM + 6, :], wgu_ref, wd_ref)


def _sample_post_call(x1, ornn, o, og, gates, vecs, wbr, wbg, wo, wgu, wd):
    n = x1.shape[0]
    args = (x1, ornn, o, og, gates, vecs, wbr, wbg, wo, wgu, wd)
    return pl.pallas_call(
        _sample_post_kernel,
        grid=(1,),
        in_specs=[_const_spec(a.shape) for a in args],
        out_specs=pl.BlockSpec((n, D_MODEL), lambda i: (0, 0)),
        out_shape=jax.ShapeDtypeStruct((n, D_MODEL), f32),
        compiler_params=pltpu.CompilerParams(dimension_semantics=("arbitrary",), vmem_limit_bytes=VMEM_LIMIT),
        name="sample_post",
    )(*args)


def _pack_kernel(w_ref, o_ref):
    o_ref[...] = pltpu.bitcast(w_ref[...].astype(bf16), jnp.uint32)


def _pack(w, *, col0=0, ncols=None, name):
    nl, k, n = w.shape
    ncols = n if ncols is None else ncols
    bn = PACK_COLS if ncols % PACK_COLS == 0 else ncols
    assert col0 % bn == 0 and ncols % bn == 0
    return pl.pallas_call(
        _pack_kernel,
        grid=(nl, ncols // bn),
        in_specs=[pl.BlockSpec((None, k, bn), lambda l, j: (l, 0, j + col0 // bn))],
        out_specs=pl.BlockSpec((None, k // 2, bn), lambda l, j: (l, 0, j)),
        out_shape=jax.ShapeDtypeStruct((nl, k // 2, ncols), jnp.uint32),
        compiler_params=pltpu.CompilerParams(dimension_semantics=("arbitrary", "arbitrary"),
                                             vmem_limit_bytes=VMEM_LIMIT),
        name=name,
    )(w)


def _pack_t_kernel(wt_ref, o_ref, *, keep):
    w = jnp.transpose(wt_ref[...])
    if keep < w.shape[1]:
        w = jnp.where(lax.broadcasted_iota(jnp.int32, w.shape, 1) < keep, w, 0.0)
    o_ref[...] = pltpu.bitcast(w.astype(bf16), jnp.uint32)


def _pack_t(wt, *, row0, nrows, keep=None, name):
    _, _, k = wt.shape
    bn = min(PACK_COLS, nrows)
    assert nrows % bn == 0 and row0 % SUBLANES == 0
    return pl.pallas_call(
        functools.partial(_pack_t_kernel, keep=nrows if keep is None else keep),
        grid=(nrows // bn,),
        in_specs=[pl.BlockSpec((None, pl.Element(bn), pl.Element(k)),
                               lambda j: (0, pl.multiple_of(row0 + j * bn, SUBLANES), 0))],
        out_specs=pl.BlockSpec((k // 2, bn), lambda j: (0, j)),
        out_shape=jax.ShapeDtypeStruct((k // 2, nrows), jnp.uint32),
        compiler_params=pltpu.CompilerParams(dimension_semantics=("arbitrary",), vmem_limit_bytes=VMEM_LIMIT),
        name=name,
    )(wt)


def _block_diag_gates(w_a, w_x):
    def bd(w):
        w = w.reshape(RNN_BLOCKS // 2, 2, RNN_BW, RNN_BW)
        z = jnp.zeros_like(w[:, 0])
        top = jnp.concatenate([w[:, 0], z], axis=2)
        bot = jnp.concatenate([z, w[:, 1]], axis=2)
        return jnp.concatenate([top, bot], axis=1)
    return jnp.concatenate([bd(w_a), bd(w_x)], axis=2)


def kernel(x_prompt, x_sample, state_rnn_h, state_rnn_conv, state_gla, norm_gains, ffn1_w_gu, ffn1_w_down, w_in, conv_w, conv_b, rg_w_a, rg_b_a, rg_w_x, rg_b_x, rg_lambda, gla_w_lr, gla_b_lr, gla_norm_g, w_branch_rnn, w_branch_gla, w_out, ffn2_w_gu, ffn2_w_down):
    assert w_in.shape == (1, D_MODEL, D_IN) and x_sample.shape[1] == 1
    bsz, seq, _ = x_prompt.shape
    nsmp = x_sample.shape[0]

    vecs = jnp.concatenate([
        norm_gains[0], conv_w[0], conv_b, rg_b_a, rg_b_x, rg_lambda,
        jnp.pad(gla_b_lr, ((0, 0), (0, D_MODEL - D_QK))), jnp.tile(gla_norm_g, (1, GLA_HEADS))], axis=0).astype(f32)
    w1gu = _pack(ffn1_w_gu, name="pack_w1gu")[0]
    w1d = _pack(ffn1_w_down, name="pack_w1d")[0]
    w2gu = _pack(ffn2_w_gu, name="pack_w2gu")[0]
    w2d = _pack(ffn2_w_down, name="pack_w2d")[0]
    w_in_t = jnp.swapaxes(w_in, 1, 2)
    wmain = _pack_t(w_in_t, row0=0, nrows=OFF_LR, name="pack_wmain")
    wlr = _pack_t(w_in_t, row0=OFF_LR, nrows=LANES, keep=GLA_RANK, name="pack_wlr")
    wlr2 = _pack(jnp.pad(gla_w_lr, ((0, 0), (0, LANES - GLA_RANK), (0, 0))), name="pack_wlr2")[0]
    wg = _pack_t(w_in_t, row0=OFF_GATES, nrows=2 * D_MODEL, name="pack_wg")
    rgbd = _pack(_block_diag_gates(rg_w_a[0], rg_w_x[0]), name="pack_rgbd")
    wbr = _pack(w_branch_rnn, name="pack_wbr")[0]
    wbg = _pack(w_branch_gla, name="pack_wbg")[0]
    wo = _pack(w_out, name="pack_wo")[0]

    xp = x_prompt.reshape(bsz * seq, D_MODEL)
    x1 = _ffn_call(xp, vecs, w1gu, w1d, row_pre=R_NORM, tm=512, name="ffn1_prompt")
    x2, hp, cp, sp = _mix_prompt_call(x1, vecs, wmain, wlr, wlr2, wg, rgbd, wbr, wbg, wo, bsz=bsz, seq=seq)
    yp = _ffn_call(x2, vecs, w2gu, w2d, row_pre=R_NORM + 4, tm=512, name="ffn2_prompt")

    xs = x_sample.reshape(nsmp, D_MODEL)
    c0 = jnp.swapaxes(state_rnn_conv[0], 0, 1)
    x1s, hs, cs, ornn, qkvg, glog, gates = _sample_pre_call(
        xs, state_rnn_h[0], c0, vecs, w1gu, w1d, wmain, wlr, wlr2, wg, rgbd)
    ss, osmp = _sample_gla_call(qkvg[:, 0:D_QK], qkvg[:, D_QK:2 * D_QK], qkvg[:, 2 * D_QK:2 * D_QK + D_V], glog,
                                state_gla[0])
    ys = _sample_post_call(x1s, ornn, osmp, qkvg[:, 2 * D_QK + D_V:], gates, vecs, wbr, wbg, wo, w2gu, w2d)

    return (yp.reshape(bsz, seq, D_MODEL), ys.reshape(nsmp, 1, D_MODEL),
            hp.reshape(1, bsz, D_RNN), cp[None], sp[None],
            hs[None], jnp.swapaxes(cs, 0, 1)[None], ss[None])
```

```python
import functools
import math

import jax
import jax.numpy as jnp
from jax import lax
from jax.experimental import pallas as pl
from jax.experimental.pallas import tpu as pltpu

D_MODEL = 1024
D_RNN = 1024
RNN_BLOCKS = 8
RNN_BW = D_RNN // RNN_BLOCKS
CONV_W = 4
RG_C = 8.0
GLA_HEADS = 4
GLA_DK = 128
GLA_DV = 256
GLA_RANK = 16
GLA_TAU = 16.0
GLA_CHUNK = 64
D_FF = 2816
EPS = 1e-6

D_QK = GLA_HEADS * GLA_DK
D_V = GLA_HEADS * GLA_DV
OFF_Q = 2 * D_RNN
OFF_K = OFF_Q + D_QK
OFF_V = OFF_K + D_QK
OFF_OG = OFF_V + D_V
OFF_LR = OFF_OG + D_V
OFF_GATES = OFF_LR + GLA_RANK
D_IN = OFF_GATES + 2 * D_MODEL

LANES = 128
SUBLANES = 8
VMEM_LIMIT = 56 * 1024 * 1024
PACK_COLS = 512
ROW_CHUNK = 32
GLA_BLOCK = 256

R_NORM = 0
R_CONVW = 6
R_CONVB = 10
R_BA = 11
R_BX = 12
R_LAM = 13
R_BLR = 14
R_GNORM = 15

bf16 = jnp.bfloat16
f32 = jnp.float32


def _w(packed):
    return pltpu.bitcast(packed, bf16)


def _dot(a, b):
    return jnp.dot(a, b, preferred_element_type=f32)


def _rms(x, g):
    return x * lax.rsqrt(jnp.mean(x * x, axis=-1, keepdims=True) + EPS) * g


def _sigmoid(x):
    return 0.5 * jnp.tanh(0.5 * x) + 0.5


def _silu(x):
    hx = 0.5 * x
    return hx * jnp.tanh(hx) + hx


def _softplus(x):
    return jnp.maximum(x, 0.0) + jnp.log1p(jnp.exp(-jnp.abs(x)))


def _sqrt_nonneg(m):
    return jnp.where(m > 0.0, m * lax.rsqrt(m), 0.0)


def _gelu_tanh(x):
    c = math.sqrt(2.0 / math.pi)
    hx = 0.5 * x
    return hx + hx * jnp.tanh(x * (c + (0.044715 * c) * (x * x)))


def _ffn(x, g_pre, g_post, wgu_ref, wd_ref):
    u = _rms(x, g_pre).astype(bf16)
    gu = _dot(u, _w(wgu_ref[...]))
    h = (_silu(gu[:, :D_FF]) * gu[:, D_FF:]).astype(bf16)
    y = _dot(h, _w(wd_ref[...]))
    return x + 0.5 * _rms(y, g_post)


def _rglru_preact(xc, rgbd_ref):
    xcb = xc.astype(bf16)
    slab = 2 * RNN_BW
    ra, ri = [], []
    for s in range(D_RNN // slab):
        gi = _dot(xcb[:, s * slab:(s + 1) * slab], _w(rgbd_ref[s]))
        ra.append(gi[:, :slab])
        ri.append(gi[:, slab:])
    return jnp.concatenate(ra, axis=1), jnp.concatenate(ri, axis=1)


def _rglru_coeffs(ra, ri, vec_ref, cols=slice(None)):
    r = _sigmoid(ra + vec_ref[R_BA:R_BA + 1, cols])
    i = _sigmoid(ri + vec_ref[R_BX:R_BX + 1, cols])
    log_a = -RG_C * r * _softplus(-vec_ref[R_LAM:R_LAM + 1, cols])
    a = jnp.exp(log_a)
    mult = _sqrt_nonneg(jnp.maximum(-(jnp.tanh(log_a) * (a * a + 1.0)), 0.0))
    return a, mult, i


def _head_norm(oh, vec_ref, h):
    gh = vec_ref[R_GNORM:R_GNORM + 1, h * GLA_DV:(h + 1) * GLA_DV]
    return oh * lax.rsqrt(jnp.mean(oh * oh, axis=-1, keepdims=True) + EPS) * gh


def _decay_log(u, wlr_ref, wlr2_ref, vec_ref):
    lr = _dot(u, _w(wlr_ref[...]))
    z = _dot(lr.astype(bf16), _w(wlr2_ref[...])) + vec_ref[R_BLR:R_BLR + 1, 0:D_QK]
    softplus_neg = jnp.maximum(-z, 0.0) + jnp.log(1.0 + jnp.exp(-jnp.abs(z)))
    return -softplus_neg * (1.0 / GLA_TAU)


def _ffn_kernel(xp_ref, xs_ref, vec_ref, wgu_ref, wd_ref, op_ref, os_ref, *, row_pre, nsteps):
    i = pl.program_id(0)
    g_pre = vec_ref[row_pre:row_pre + 1, :]
    g_post = vec_ref[row_pre + 1:row_pre + 2, :]

    @pl.when(i < nsteps)
    def _():
        op_ref[...] = _ffn(xp_ref[...], g_pre, g_post, wgu_ref, wd_ref)

    @pl.when(i == nsteps)
    def _():
        os_ref[...] = _ffn(xs_ref[...], g_pre, g_post, wgu_ref, wd_ref)


def _const_spec(shape):
    nd = len(shape)
    return pl.BlockSpec(shape, lambda *_: (0,) * nd, pipeline_mode=pl.Buffered(1))


def _ffn_call(xp, xs, vecs, wgu, wd, *, row_pre, tm, name):
    n = xp.shape[0]
    nsteps = n // tm
    tile = lambda i: (jnp.minimum(i, nsteps - 1), 0)
    return pl.pallas_call(
        functools.partial(_ffn_kernel, row_pre=row_pre, nsteps=nsteps),
        grid=(nsteps + 1,),
        in_specs=[pl.BlockSpec((tm, D_MODEL), tile), _const_spec(xs.shape),
                  _const_spec(vecs.shape), _const_spec(wgu.shape), _const_spec(wd.shape)],
        out_specs=[pl.BlockSpec((tm, D_MODEL), tile), pl.BlockSpec(xs.shape, lambda i: (0, 0))],
        out_shape=[jax.ShapeDtypeStruct((n, D_MODEL), f32), jax.ShapeDtypeStruct(xs.shape, f32)],
        compiler_params=pltpu.CompilerParams(dimension_semantics=("arbitrary",), vmem_limit_bytes=VMEM_LIMIT),
        name=name,
    )(xp, xs, vecs, wgu, wd)


def _gla_prepare(q, k, v, glog):
    rows = q.shape[0]
    c = GLA_CHUNK
    shift = c.bit_length() - 1
    ri = lax.broadcasted_iota(jnp.int32, (rows, rows), 0)
    ci = lax.broadcasted_iota(jnp.int32, (rows, rows), 1)
    mask = jnp.logical_and(lax.shift_right_logical(ri, shift) == lax.shift_right_logical(ci, shift), ri >= ci)
    ltri = mask.astype(bf16)
    g1 = glog.astype(bf16)
    r1 = glog - g1.astype(f32)
    g2 = r1.astype(bf16)
    g3 = (r1 - g2.astype(f32)).astype(bf16)
    bcum = _dot(ltri, g1) + _dot(ltri, g2) + _dot(ltri, g3)
    blast = [bcum[(n + 1) * c - 1:(n + 1) * c, :] for n in range(rows // c)]
    blast_b = jnp.concatenate([jnp.broadcast_to(b, (c, D_QK)) for b in blast], axis=0)
    qi = (q * (GLA_DK ** -0.5) * jnp.exp(bcum)).astype(bf16)
    ki = (k * jnp.exp(-bcum)).astype(bf16)
    kend = (k * jnp.exp(blast_b - bcum)).astype(bf16)
    dec = [jnp.exp(b) for b in blast]
    return qi, ki, kend, v.astype(bf16), dec, mask


def _gla_intra(qi, ki, vb, mask, h):
    ks = slice(h * GLA_DK, (h + 1) * GLA_DK)
    vs = slice(h * GLA_DV, (h + 1) * GLA_DV)
    att = lax.dot_general(qi[:, ks], ki[:, ks], (((1,), (1,)), ((), ())), preferred_element_type=f32)
    att = jnp.where(mask, att, 0.0).astype(bf16)
    return _dot(att, vb[:, vs])


def _gla_inter(qi, kend, vb, dec, S, h, reset):
    c = GLA_CHUNK
    ks = slice(h * GLA_DK, (h + 1) * GLA_DK)
    vs = slice(h * GLA_DV, (h + 1) * GLA_DV)
    nch = qi.shape[0] // c
    upd, dcol = [], []
    for n in range(nch):
        rs = slice(n * c, (n + 1) * c)
        upd.append(lax.dot_general(kend[rs, ks], vb[rs, vs], (((0,), (0,)), ((), ())), preferred_element_type=f32))
        d = jnp.transpose(jnp.broadcast_to(dec[n][:, ks], (GLA_DK, GLA_DK)))
        dcol.append(jnp.concatenate([d, d], axis=1))
    sh = jnp.where(reset, 0.0, S[h])
    outs = []
    for n in range(nch):
        rs = slice(n * c, (n + 1) * c)
        outs.append(_dot(qi[rs, ks], sh.astype(bf16)))
        sh = sh * dcol[n] + upd[n]
    S[h] = sh
    return jnp.concatenate(outs, axis=0)


def _interleave(order, **gens):
    def step(k):
        if k in gens:
            try:
                next(gens[k])
            except StopIteration:
                del gens[k]
    for k in order:
        step(k)
    while gens:
        for k in list(gens):
            step(k)


def _mix_prompt_kernel(xp_ref, xa_ref, vec_ref, wmain_ref, wlr_ref, wlr2_ref, wg_ref, rgbd_ref, wbr_ref, wbg_ref,
                       wo_ref, y_ref, h_ref, conv_ref, s_ref,
                       u_s, xrb, yr_s, ornn_s, hc, tail, S, hbuf, pre_s, *, bpb):
    j = pl.program_id(0)
    rb = GLA_BLOCK
    sw = 2 * RNN_BW
    nw = 512

    @pl.when(j == 0)
    def _():
        u_s[...] = jnp.zeros_like(u_s)
        xrb[...] = jnp.zeros_like(xrb)
        yr_s[...] = jnp.zeros_like(yr_s)
        ornn_s[...] = jnp.zeros_like(ornn_s)
        hc[...] = jnp.zeros_like(hc)
        tail[...] = jnp.zeros_like(tail)
        S[...] = jnp.zeros_like(S)

    p3 = j % 3
    p2 = j % 2
    r2 = (j + 1) % 2
    a3 = (j + 1) % 3
    r_first = (j + bpb - 1) % bpb == 0
    a_first = (j + 2 * bpb - 2) % bpb == 0

    def project():
        u = _rms(xp_ref[...], vec_ref[R_NORM + 2:R_NORM + 3, :]).astype(bf16)
        u_s[p3] = u
        yield
        for c0 in range(0, D_RNN, sw):
            xrb[p2, SUBLANES:SUBLANES + rb, c0:c0 + sw] = _dot(u, _w(wmain_ref[:, c0:c0 + sw]))
            yield
        for c0 in range(0, D_RNN, nw):
            yr_s[p2, :, c0:c0 + nw] = _dot(u, _w(wmain_ref[:, D_RNN + c0:D_RNN + c0 + nw]))
            yield

    def recur():
        xrb[r2, 0:SUBLANES, :] = jnp.where(r_first, 0.0, tail[...])
        rc = ROW_CHUNK
        sub = lax.broadcasted_iota(jnp.int32, (SUBLANES, sw), 0)
        scan_steps = [(sh, sub >= sh) for sh in (1, 2, 4)]
        first_row = jnp.logical_and(sub == 0, r_first)
        for s in range(D_RNN // sw):
            cs = slice(s * sw, (s + 1) * sw)
            for r0 in range(0, rb, rc):
                ext = xrb[r2, r0:r0 + SUBLANES + rc, cs]
                xc = vec_ref[R_CONVB:R_CONVB + 1, cs]
                for t in range(CONV_W):
                    lag = CONV_W - 1 - t
                    tap = ext if lag == 0 else pltpu.roll(ext, lag, 0)
                    xc = xc + tap[SUBLANES:, :] * vec_ref[R_CONVW + t:R_CONVW + t + 1, cs]
                hbuf[r0:r0 + rc, cs] = xc
            yield
            pre_s[s] = _dot(hbuf[:, cs].astype(bf16), _w(rgbd_ref[s]))
            yield
            hprev = jnp.where(r_first, 0.0, hc[:, cs])
            for r0 in range(0, rb, rc):
                hs = []
                for g0 in range(r0, r0 + rc, SUBLANES):
                    gs = slice(g0, g0 + SUBLANES)
                    xc = hbuf[gs, cs]
                    a, mult, gi = _rglru_coeffs(pre_s[s, gs, 0:sw], pre_s[s, gs, sw:2 * sw], vec_ref, cs)
                    if g0 == 0:
                        mult = jnp.where(first_row, 1.0, mult)
                    b = mult * gi * xc
                    for sh, m in scan_steps:
                        b = jnp.where(m, a * pltpu.roll(b, sh, 0) + b, b)
                        a = jnp.where(m, a * pltpu.roll(a, sh, 0), a)
                    hg = a * hprev + b
                    hs.append(hg)
                    hprev = jnp.broadcast_to(hg[SUBLANES - 1:SUBLANES, :], (SUBLANES, sw))
                h = jnp.concatenate(hs, axis=0)
                ornn_s[r2, r0:r0 + rc, cs] = (h * _gelu_tanh(yr_s[r2, r0:r0 + rc, cs])).astype(bf16)
                if (r0 // rc) % 2 == 1:
                    yield
            hc[:, cs] = hprev
        tail[...] = xrb[r2, rb:rb + SUBLANES, :]

    def attend():
        u = u_s[a3]
        o_rnn = ornn_s[p2]
        proj = []
        for c0 in range(OFF_Q, OFF_OG, nw):
            proj.append(_dot(u, _w(wmain_ref[:, c0:c0 + nw])))
            yield
        proj = jnp.concatenate(proj, axis=1)
        glog = _decay_log(u, wlr_ref, wlr2_ref, vec_ref)
        yield
        qi, ki, kend, vb, dec, mask = _gla_prepare(proj[:, 0:D_QK], proj[:, D_QK:2 * D_QK], proj[:, 2 * D_QK:], glog)
        yield
        og = []
        for c0 in range(OFF_OG, OFF_LR, nw):
            og.append(_dot(u, _w(wmain_ref[:, c0:c0 + nw])))
            yield
        og = jnp.concatenate(og, axis=1)
        gates = []
        o_heads = []
        gate_cols = iter(range(0, 2 * D_MODEL, nw))
        for h in range(GLA_HEADS):
            hs = slice(h * GLA_DV, (h + 1) * GLA_DV)
            oh = _gla_intra(qi, ki, vb, mask, h) + _gla_inter(qi, kend, vb, dec, S, h, a_first)
            o_heads.append((_head_norm(oh, vec_ref, h) * _silu(og[:, hs])).astype(bf16))
            yield
            for _ in range(2 * D_MODEL // nw // GLA_HEADS):
                c0 = next(gate_cols)
                gates.append(_dot(u, _w(wg_ref[:, c0:c0 + nw])))
                yield
        part_a = []
        for i, c0 in enumerate(range(0, D_MODEL, nw)):
            part_a.append(_sigmoid(gates[i]) * _dot(o_rnn, _w(wbr_ref[:, c0:c0 + nw])))
            yield
        o_gla = jnp.concatenate(o_heads, axis=1)
        merged = []
        ng = D_MODEL // nw
        for i, c0 in enumerate(range(0, D_MODEL, nw)):
            yb = _dot(o_gla, _w(wbg_ref[:, c0:c0 + nw]))
            merged.append((part_a[i] + _sigmoid(gates[ng + i]) * yb).astype(bf16))
            yield
        merged = jnp.concatenate(merged, axis=1)
        mix = []
        for c0 in range(0, D_MODEL, nw):
            mix.append(_dot(merged, _w(wo_ref[:, c0:c0 + nw])))
            yield
        mix = jnp.concatenate(mix, axis=1)
        y_ref[...] = xa_ref[...] + _rms(mix, vec_ref[R_NORM + 3:R_NORM + 4, :])
        yield

    _interleave("", A=attend(), R=recur(), P=project())

    @pl.when((j + bpb - 1) % bpb == bpb - 1)
    def _():
        h_ref[...] = hc[0:1, :]
        conv_ref[...] = tail[SUBLANES - (CONV_W - 1):SUBLANES, :]

    @pl.when((j + 2 * bpb - 2) % bpb == bpb - 1)
    def _():
        s_ref[...] = S[...]


def _mix_prompt_call(x2d, vecs, wmain, wlr, wlr2, wg, rgbd, wbr, wbg, wo, *, bsz, seq):
    rb = GLA_BLOCK
    bpb = seq // rb
    nblk = bsz * bpb
    consts = (vecs, wmain, wlr, wlr2, wg, rgbd, wbr, wbg, wo)

    def blk(d):
        return lambda j: (jnp.clip(j - d, 0, nblk - 1), 0)

    def seq_of(d, nd):
        return lambda j: (jnp.clip(j - d, 0, nblk - 1) // bpb,) + (0,) * nd

    return pl.pallas_call(
        functools.partial(_mix_prompt_kernel, bpb=bpb),
        grid=(nblk + 2,),
        in_specs=[pl.BlockSpec((rb, D_MODEL), blk(0)), pl.BlockSpec((rb, D_MODEL), blk(2))]
        + [_const_spec(w.shape) for w in consts],
        out_specs=[pl.BlockSpec((rb, D_MODEL), blk(2)),
                   pl.BlockSpec((None, 1, D_RNN), seq_of(1, 2)),
                   pl.BlockSpec((None, CONV_W - 1, D_RNN), seq_of(1, 2)),
                   pl.BlockSpec((None, GLA_HEADS, GLA_DK, GLA_DV), seq_of(2, 3))],
        out_shape=[jax.ShapeDtypeStruct((nblk * rb, D_MODEL), f32),
                   jax.ShapeDtypeStruct((bsz, 1, D_RNN), f32),
                   jax.ShapeDtypeStruct((bsz, CONV_W - 1, D_RNN), f32),
                   jax.ShapeDtypeStruct((bsz, GLA_HEADS, GLA_DK, GLA_DV), f32)],
        scratch_shapes=[pltpu.VMEM((3, rb, D_MODEL), bf16),
                        pltpu.VMEM((2, rb + SUBLANES, D_RNN), f32),
                        pltpu.VMEM((2, rb, D_RNN), f32),
                        pltpu.VMEM((2, rb, D_RNN), bf16),
                        pltpu.VMEM((SUBLANES, D_RNN), f32),
                        pltpu.VMEM((SUBLANES, D_RNN), f32),
                        pltpu.VMEM((GLA_HEADS, GLA_DK, GLA_DV), f32),
                        pltpu.VMEM((rb, D_RNN), f32),
                        pltpu.VMEM((D_RNN // (2 * RNN_BW), rb, 4 * RNN_BW), f32)],
        compiler_params=pltpu.CompilerParams(dimension_semantics=("arbitrary",), vmem_limit_bytes=VMEM_LIMIT),
        name="mix_prompt",
    )(x2d, x2d, *consts)


def _sample_pre_kernel(x1_ref, h0_ref, c0_ref, vec_ref, wmain_ref, wlr_ref, wlr2_ref, wg_ref, rgbd_ref,
                       hn_ref, cn_ref, ornn_ref, qkvg_ref, glog_ref, gates_ref):
    u = _rms(x1_ref[...], vec_ref[R_NORM + 2:R_NORM + 3, :]).astype(bf16)
    xy = _dot(u, _w(wmain_ref[:, 0:2 * D_RNN]))
    xr = xy[:, :D_RNN]
    yr = xy[:, D_RNN:]
    xc = vec_ref[R_CONVB:R_CONVB + 1, :]
    for j in range(CONV_W - 1):
        xc = xc + c0_ref[j] * vec_ref[R_CONVW + j:R_CONVW + j + 1, :]
    xc = xc + xr * vec_ref[R_CONVW + CONV_W - 1:R_CONVW + CONV_W, :]
    for j in range(CONV_W - 2):
        cn_ref[j] = c0_ref[j + 1]
    cn_ref[CONV_W - 2] = xr
    ra, ri = _rglru_preact(xc, rgbd_ref)
    a, mult, gi = _rglru_coeffs(ra, ri, vec_ref)
    h = a * h0_ref[...] + mult * gi * xc
    hn_ref[...] = h
    ornn_ref[...] = (h * _gelu_tanh(yr)).astype(bf16)
    qkvg_ref[...] = _dot(u, _w(wmain_ref[:, OFF_Q:OFF_LR]))
    glog_ref[...] = _decay_log(u, wlr_ref, wlr2_ref, vec_ref)
    gates_ref[...] = _dot(u, _w(wg_ref[...]))


def _sample_pre_call(x1, h0, c0, vecs, wmain, wlr, wlr2, wg, rgbd):
    n = x1.shape[0]
    args = (x1, h0, c0, vecs, wmain, wlr, wlr2, wg, rgbd)
    outs = [((n, D_RNN), f32), ((CONV_W - 1, n, D_RNN), f32), ((n, D_RNN), bf16),
            ((n, OFF_LR - OFF_Q), f32), ((n, D_QK), f32), ((n, 2 * D_MODEL), f32)]
    return pl.pallas_call(
        _sample_pre_kernel,
        grid=(1,),
        in_specs=[_const_spec(a.shape) for a in args],
        out_specs=[pl.BlockSpec(s, lambda i, nd=len(s): (0,) * nd) for s, _ in outs],
        out_shape=[jax.ShapeDtypeStruct(s, d) for s, d in outs],
        compiler_params=pltpu.CompilerParams(dimension_semantics=("arbitrary",), vmem_limit_bytes=VMEM_LIMIT),
        name="sample_pre",
    )(*args)


def _col_bcast(rows8, width):
    tiled = jnp.concatenate([rows8] * (LANES // SUBLANES), axis=0)
    tt = jnp.transpose(tiled)
    return [jnp.broadcast_to(tt[:, j:j + 1], (LANES, width)) for j in range(SUBLANES)]


def _sample_gla_kernel(q_ref, k_ref, v_ref, g_ref, s0_ref, sn_ref, o_ref):
    scale = GLA_DK ** -0.5
    for h in range(GLA_HEADS):
        ks = slice(h * GLA_DK, (h + 1) * GLA_DK)
        vs = slice(h * GLA_DV, (h + 1) * GLA_DV)
        dcols = _col_bcast(jnp.exp(g_ref[:, ks]), GLA_DV)
        kcols = _col_bcast(k_ref[:, ks], GLA_DV)
        q8 = (q_ref[:, ks] * scale).astype(bf16)
        for j in range(SUBLANES):
            vrow = v_ref[j:j + 1, vs]
            sn = dcols[j] * s0_ref[j, h] + kcols[j] * vrow
            sn_ref[j, h] = sn
            o_ref[j:j + 1, vs] = _dot(q8, sn.astype(bf16))[j:j + 1, :]


def _sample_gla_call(q, k, v, g, s0):
    n = q.shape[0]
    sb = (SUBLANES, GLA_HEADS, GLA_DK, GLA_DV)
    return pl.pallas_call(
        _sample_gla_kernel,
        grid=(n // SUBLANES,),
        in_specs=[pl.BlockSpec((SUBLANES, D_QK), lambda i: (i, 0)),
                  pl.BlockSpec((SUBLANES, D_QK), lambda i: (i, 0)),
                  pl.BlockSpec((SUBLANES, D_V), lambda i: (i, 0)),
                  pl.BlockSpec((SUBLANES, D_QK), lambda i: (i, 0)),
                  pl.BlockSpec(sb, lambda i: (i, 0, 0, 0))],
        out_specs=[pl.BlockSpec(sb, lambda i: (i, 0, 0, 0)),
                   pl.BlockSpec((SUBLANES, D_V), lambda i: (i, 0))],
        out_shape=[jax.ShapeDtypeStruct(s0.shape, f32), jax.ShapeDtypeStruct((n, D_V), f32)],
        compiler_params=pltpu.CompilerParams(dimension_semantics=("arbitrary",), vmem_limit_bytes=VMEM_LIMIT),
        name="sample_gla",
    )(q, k, v, g, s0)


def _sample_post_kernel(x1_ref, ornn_ref, o_ref, og_ref, gates_ref, vec_ref, wbr_ref, wbg_ref, wo_ref, x2_ref):
    o = o_ref[...]
    og = og_ref[...]
    o_gla = jnp.concatenate(
        [_head_norm(o[:, h * GLA_DV:(h + 1) * GLA_DV], vec_ref, h) * _silu(og[:, h * GLA_DV:(h + 1) * GLA_DV])
         for h in range(GLA_HEADS)], axis=1).astype(bf16)
    gates = gates_ref[...]
    ya = _dot(ornn_ref[...], _w(wbr_ref[...]))
    yb = _dot(o_gla, _w(wbg_ref[...]))
    merged = _sigmoid(gates[:, :D_MODEL]) * ya + _sigmoid(gates[:, D_MODEL:]) * yb
    mix = _dot(merged.astype(bf16), _w(wo_ref[...]))
    x2_ref[...] = x1_ref[...] + _rms(mix, vec_ref[R_NORM + 3:R_NORM + 4, :])


def _sample_post_call(x1, ornn, o, og, gates, vecs, wbr, wbg, wo):
    n = x1.shape[0]
    args = (x1, ornn, o, og, gates, vecs, wbr, wbg, wo)
    return pl.pallas_call(
        _sample_post_kernel,
        grid=(1,),
        in_specs=[_const_spec(a.shape) for a in args],
        out_specs=pl.BlockSpec((n, D_MODEL), lambda i: (0, 0)),
        out_shape=jax.ShapeDtypeStruct((n, D_MODEL), f32),
        compiler_params=pltpu.CompilerParams(dimension_semantics=("arbitrary",), vmem_limit_bytes=VMEM_LIMIT),
        name="sample_post",
    )(*args)


def _pack_kernel(w_ref, o_ref):
    o_ref[...] = pltpu.bitcast(w_ref[...].astype(bf16), jnp.uint32)


def _pack(w, *, bn=None, name):
    nl, k, n = w.shape
    if bn is None:
        bn = PACK_COLS if n % PACK_COLS == 0 else n
    assert n % bn == 0
    return pl.pallas_call(
        _pack_kernel,
        grid=(nl, n // bn),
        in_specs=[pl.BlockSpec((None, k, bn), lambda l, j: (l, 0, j))],
        out_specs=pl.BlockSpec((None, k // 2, bn), lambda l, j: (l, 0, j)),
        out_shape=jax.ShapeDtypeStruct((nl, k // 2, n), jnp.uint32),
        compiler_params=pltpu.CompilerParams(dimension_semantics=("arbitrary", "arbitrary"),
                                             vmem_limit_bytes=VMEM_LIMIT),
        name=name,
    )(w)


def _pack_t_kernel(wt_ref, o_ref, *, keep):
    w = jnp.transpose(wt_ref[...])
    if keep < w.shape[1]:
        w = jnp.where(lax.broadcasted_iota(jnp.int32, w.shape, 1) < keep, w, 0.0)
    o_ref[...] = pltpu.bitcast(w.astype(bf16), jnp.uint32)


def _pack_t(wt, *, row0, nrows, keep=None, name):
    _, _, k = wt.shape
    bn = min(PACK_COLS, nrows)
    assert nrows % bn == 0 and row0 % SUBLANES == 0
    return pl.pallas_call(
        functools.partial(_pack_t_kernel, keep=nrows if keep is None else keep),
        grid=(nrows // bn,),
        in_specs=[pl.BlockSpec((None, pl.Element(bn), pl.Element(k)),
                               lambda j: (0, pl.multiple_of(row0 + j * bn, SUBLANES), 0))],
        out_specs=pl.BlockSpec((k // 2, bn), lambda j: (0, j)),
        out_shape=jax.ShapeDtypeStruct((k // 2, nrows), jnp.uint32),
        compiler_params=pltpu.CompilerParams(dimension_semantics=("arbitrary",), vmem_limit_bytes=VMEM_LIMIT),
        name=name,
    )(wt)


def _block_diag_gates(w_a, w_x):
    def bd(w):
        w = w.reshape(RNN_BLOCKS // 2, 2, RNN_BW, RNN_BW)
        z = jnp.zeros_like(w[:, 0])
        top = jnp.concatenate([w[:, 0], z], axis=2)
        bot = jnp.concatenate([z, w[:, 1]], axis=2)
        return jnp.concatenate([top, bot], axis=1)
    return jnp.concatenate([bd(w_a), bd(w_x)], axis=2)


def kernel(x_prompt, x_sample, state_rnn_h, state_rnn_conv, state_gla, norm_gains, ffn1_w_gu, ffn1_w_down, w_in, conv_w, conv_b, rg_w_a, rg_b_a, rg_w_x, rg_b_x, rg_lambda, gla_w_lr, gla_b_lr, gla_norm_g, w_branch_rnn, w_branch_gla, w_out, ffn2_w_gu, ffn2_w_down):
    assert w_in.shape == (1, D_MODEL, D_IN) and x_sample.shape[1] == 1
    bsz, seq, _ = x_prompt.shape
    nsmp = x_sample.shape[0]

    vecs = jnp.concatenate([
        norm_gains[0], conv_w[0], conv_b, rg_b_a, rg_b_x, rg_lambda,
        jnp.pad(gla_b_lr, ((0, 0), (0, D_MODEL - D_QK))), jnp.tile(gla_norm_g, (1, GLA_HEADS))], axis=0).astype(f32)
    w1gu = _pack(ffn1_w_gu, bn=D_FF // 2, name="pack_w1gu")[0]
    w1d = _pack(ffn1_w_down, name="pack_w1d")[0]
    w2gu = _pack(ffn2_w_gu, bn=D_FF // 2, name="pack_w2gu")[0]
    w2d = _pack(ffn2_w_down, name="pack_w2d")[0]
    w_in_t = jnp.swapaxes(w_in, 1, 2)
    wmain = _pack_t(w_in_t, row0=0, nrows=OFF_LR, name="pack_wmain")
    wlr = _pack_t(w_in_t, row0=OFF_LR, nrows=LANES, keep=GLA_RANK, name="pack_wlr")
    wlr2 = _pack(jnp.pad(gla_w_lr, ((0, 0), (0, LANES - GLA_RANK), (0, 0))), name="pack_wlr2")[0]
    wg = _pack_t(w_in_t, row0=OFF_GATES, nrows=2 * D_MODEL, name="pack_wg")
    rgbd = _pack(_block_diag_gates(rg_w_a[0], rg_w_x[0]), name="pack_rgbd")
    wbr = _pack(w_branch_rnn, name="pack_wbr")[0]
    wbg = _pack(w_branch_gla, name="pack_wbg")[0]
    wo = _pack(w_out, name="pack_wo")[0]

    xp = x_prompt.reshape(bsz * seq, D_MODEL)
    xs = x_sample.reshape(nsmp, D_MODEL)
    x1, x1s = _ffn_call(xp, xs, vecs, w1gu, w1d, row_pre=R_NORM, tm=512, name="ffn1")
    x2, hp, cp, sp = _mix_prompt_call(x1, vecs, wmain, wlr, wlr2, wg, rgbd, wbr, wbg, wo, bsz=bsz, seq=seq)

    c0 = jnp.swapaxes(state_rnn_conv[0], 0, 1)
    hs, cs, ornn, qkvg, glog, gates = _sample_pre_call(x1s, state_rnn_h[0], c0, vecs, wmain, wlr, wlr2, wg, rgbd)
    ss, osmp = _sample_gla_call(qkvg[:, 0:D_QK], qkvg[:, D_QK:2 * D_QK], qkvg[:, 2 * D_QK:2 * D_QK + D_V], glog,
                                state_gla[0])
    x2s = _sample_post_call(x1s, ornn, osmp, qkvg[:, 2 * D_QK + D_V:], gates, vecs, wbr, wbg, wo)

    yp, ys = _ffn_call(x2, x2s, vecs, w2gu, w2d, row_pre=R_NORM + 4, tm=512, name="ffn2")

    return (yp.reshape(bsz, seq, D_MODEL), ys.reshape(nsmp, 1, D_MODEL),
            hp.reshape(1, bsz, D_RNN), cp[None], sp[None],
            hs[None], jnp.swapaxes(cs, 0, 1)[None], ss[None])
```

```python
import functools
import math

import jax
import jax.numpy as jnp
from jax import lax
from jax.experimental import pallas as pl
from jax.experimental.pallas import tpu as pltpu

D_MODEL = 1024
D_RNN = 1024
RNN_BLOCKS = 8
RNN_BW = D_RNN // RNN_BLOCKS
CONV_W = 4
RG_C = 8.0
GLA_HEADS = 4
GLA_DK = 128
GLA_DV = 256
GLA_RANK = 16
GLA_TAU = 16.0
GLA_CHUNK = 64
D_FF = 2816
EPS = 1e-6

D_QK = GLA_HEADS * GLA_DK
D_V = GLA_HEADS * GLA_DV
OFF_Q = 2 * D_RNN
OFF_K = OFF_Q + D_QK
OFF_V = OFF_K + D_QK
OFF_OG = OFF_V + D_V
OFF_LR = OFF_OG + D_V
OFF_GATES = OFF_LR + GLA_RANK
D_IN = OFF_GATES + 2 * D_MODEL

LANES = 128
SUBLANES = 8
VMEM_LIMIT = 56 * 1024 * 1024
PACK_COLS = 512
ROW_CHUNK = 32
GLA_BLOCK = 256

R_NORM = 0
R_CONVW = 6
R_CONVB = 10
R_BA = 11
R_BX = 12
R_LAM = 13
R_BLR = 14
R_GNORM = 15

bf16 = jnp.bfloat16
f32 = jnp.float32


def _w(packed):
    return pltpu.bitcast(packed, bf16)


def _dot(a, b):
    return jnp.dot(a, b, preferred_element_type=f32)


def _rms(x, g):
    return x * lax.rsqrt(jnp.mean(x * x, axis=-1, keepdims=True) + EPS) * g


def _sigmoid(x):
    return 0.5 * jnp.tanh(0.5 * x) + 0.5


def _silu(x):
    hx = 0.5 * x
    return hx * jnp.tanh(hx) + hx


def _softplus(x):
    return jnp.maximum(x, 0.0) + jnp.log1p(jnp.exp(-jnp.abs(x)))


def _sqrt_nonneg(m):
    return jnp.where(m > 0.0, m * lax.rsqrt(m), 0.0)


def _gelu_tanh(x):
    c = math.sqrt(2.0 / math.pi)
    hx = 0.5 * x
    return hx + hx * jnp.tanh(x * (c + (0.044715 * c) * (x * x)))


def _ffn(x, g_pre, g_post, wgu_ref, wd_ref):
    u = _rms(x, g_pre).astype(bf16)
    gu = _dot(u, _w(wgu_ref[...]))
    h = (_silu(gu[:, :D_FF]) * gu[:, D_FF:]).astype(bf16)
    y = _dot(h, _w(wd_ref[...]))
    return x + 0.5 * _rms(y, g_post)


def _rglru_preact(xc, rgbd_ref):
    xcb = xc.astype(bf16)
    slab = 2 * RNN_BW
    ra, ri = [], []
    for s in range(D_RNN // slab):
        gi = _dot(xcb[:, s * slab:(s + 1) * slab], _w(rgbd_ref[s]))
        ra.append(gi[:, :slab])
        ri.append(gi[:, slab:])
    return jnp.concatenate(ra, axis=1), jnp.concatenate(ri, axis=1)


def _rglru_coeffs(ra, ri, vec_ref, cols=slice(None)):
    r = _sigmoid(ra + vec_ref[R_BA:R_BA + 1, cols])
    i = _sigmoid(ri + vec_ref[R_BX:R_BX + 1, cols])
    log_a = -RG_C * r * _softplus(-vec_ref[R_LAM:R_LAM + 1, cols])
    a = jnp.exp(log_a)
    mult = _sqrt_nonneg(jnp.maximum(-(jnp.tanh(log_a) * (a * a + 1.0)), 0.0))
    return a, mult, i


def _head_norm(oh, vec_ref, h):
    gh = vec_ref[R_GNORM:R_GNORM + 1, h * GLA_DV:(h + 1) * GLA_DV]
    return oh * lax.rsqrt(jnp.mean(oh * oh, axis=-1, keepdims=True) + EPS) * gh


def _decay_log(u, wlr_ref, wlr2_ref, vec_ref):
    lr = _dot(u, _w(wlr_ref[...]))
    z = _dot(lr.astype(bf16), _w(wlr2_ref[...])) + vec_ref[R_BLR:R_BLR + 1, 0:D_QK]
    softplus_neg = jnp.maximum(-z, 0.0) + jnp.log(1.0 + jnp.exp(-jnp.abs(z)))
    return -softplus_neg * (1.0 / GLA_TAU)


def _ffn_kernel(xp_ref, xs_ref, vec_ref, wgu_ref, wd_ref, *rest, row_pre, nsteps, side_kinds):
    ns = len(side_kinds)
    side_in, (op_ref, os_ref), side_out = rest[:ns], rest[ns:ns + 2], rest[ns + 2:]
    i = pl.program_id(0)
    g_pre = vec_ref[row_pre:row_pre + 1, :]
    g_post = vec_ref[row_pre + 1:row_pre + 2, :]

    @pl.when(i < nsteps)
    def _():
        op_ref[...] = _ffn(xp_ref[...], g_pre, g_post, wgu_ref, wd_ref)
        for kind, src, dst in zip(side_kinds, side_in, side_out):
            w = jnp.transpose(src[...]) if kind == "t" else src[...]
            dst[...] = pltpu.bitcast(w.astype(bf16), jnp.uint32)

    @pl.when(i == nsteps)
    def _():
        os_ref[...] = _ffn(xs_ref[...], g_pre, g_post, wgu_ref, wd_ref)


def _const_spec(shape):
    nd = len(shape)
    return pl.BlockSpec(shape, lambda *_: (0,) * nd, pipeline_mode=pl.Buffered(1))


def _side_pack(kind, w, b, nsteps):
    if kind == "cols":
        _, k, n = w.shape
        nb = n // b
        blk = lambda i: jnp.minimum(i, nb - 1)
        specs = (pl.BlockSpec((None, k, b), lambda i: (0, 0, blk(i))), pl.BlockSpec((k // 2, b), lambda i: (0, blk(i))))
        shape = (k // 2, n)
    elif kind == "rows":
        _, k, n = w.shape
        nb = k // b
        blk = lambda i: jnp.minimum(i, nb - 1)
        specs = (pl.BlockSpec((None, b, n), lambda i: (0, blk(i), 0)), pl.BlockSpec((b // 2, n), lambda i: (blk(i), 0)))
        shape = (k // 2, n)
    else:
        _, _, k = w.shape
        nmain = OFF_LR // b
        nb = nmain + 2 * D_MODEL // b
        blk = lambda i: jnp.minimum(i, nb - 1)
        row = lambda i: pl.multiple_of(jnp.where(blk(i) < nmain, blk(i) * b, OFF_GATES + (blk(i) - nmain) * b), SUBLANES)
        specs = (pl.BlockSpec((None, pl.Element(b), pl.Element(k)), lambda i: (0, row(i), 0)),
                 pl.BlockSpec((k // 2, b), lambda i: (0, blk(i))))
        shape = (k // 2, nb * b)
    assert nb <= nsteps
    return specs, jax.ShapeDtypeStruct(shape, jnp.uint32)


def _ffn_call(xp, xs, vecs, wgu, wd, *, row_pre, tm, name, side=()):
    n = xp.shape[0]
    nsteps = n // tm
    tile = lambda i: (jnp.minimum(i, nsteps - 1), 0)
    packs = [_side_pack(kind, w, b, nsteps) for kind, w, b in side]
    return pl.pallas_call(
        functools.partial(_ffn_kernel, row_pre=row_pre, nsteps=nsteps, side_kinds=tuple(k for k, _, _ in side)),
        grid=(nsteps + 1,),
        in_specs=[pl.BlockSpec((tm, D_MODEL), tile), _const_spec(xs.shape),
                  _const_spec(vecs.shape), _const_spec(wgu.shape), _const_spec(wd.shape)]
        + [specs[0] for specs, _ in packs],
        out_specs=[pl.BlockSpec((tm, D_MODEL), tile), pl.BlockSpec(xs.shape, lambda i: (0, 0))]
        + [specs[1] for specs, _ in packs],
        out_shape=[jax.ShapeDtypeStruct((n, D_MODEL), f32), jax.ShapeDtypeStruct(xs.shape, f32)]
        + [shape for _, shape in packs],
        compiler_params=pltpu.CompilerParams(dimension_semantics=("arbitrary",), vmem_limit_bytes=VMEM_LIMIT),
        name=name,
    )(xp, xs, vecs, wgu, wd, *[w for _, w, _ in side])


def _gla_prepare(q, k, v, glog):
    rows = q.shape[0]
    c = GLA_CHUNK
    shift = c.bit_length() - 1
    ri = lax.broadcasted_iota(jnp.int32, (rows, rows), 0)
    ci = lax.broadcasted_iota(jnp.int32, (rows, rows), 1)
    mask = jnp.logical_and(lax.shift_right_logical(ri, shift) == lax.shift_right_logical(ci, shift), ri >= ci)
    ltri = mask.astype(bf16)
    g1 = glog.astype(bf16)
    r1 = glog - g1.astype(f32)
    g2 = r1.astype(bf16)
    g3 = (r1 - g2.astype(f32)).astype(bf16)
    bcum = _dot(ltri, g1) + _dot(ltri, g2) + _dot(ltri, g3)
    blast = [bcum[(n + 1) * c - 1:(n + 1) * c, :] for n in range(rows // c)]
    blast_b = jnp.concatenate([jnp.broadcast_to(b, (c, D_QK)) for b in blast], axis=0)
    qi = (q * (GLA_DK ** -0.5) * jnp.exp(bcum)).astype(bf16)
    ki = (k * jnp.exp(-bcum)).astype(bf16)
    kend = (k * jnp.exp(blast_b - bcum)).astype(bf16)
    dec = [jnp.exp(b) for b in blast]
    return qi, ki, kend, v.astype(bf16), dec, mask


def _gla_intra(qi, ki, vb, mask, h):
    ks = slice(h * GLA_DK, (h + 1) * GLA_DK)
    vs = slice(h * GLA_DV, (h + 1) * GLA_DV)
    att = lax.dot_general(qi[:, ks], ki[:, ks], (((1,), (1,)), ((), ())), preferred_element_type=f32)
    att = jnp.where(mask, att, 0.0).astype(bf16)
    return _dot(att, vb[:, vs])


def _gla_inter(qi, kend, vb, dec, S, h, reset):
    c = GLA_CHUNK
    ks = slice(h * GLA_DK, (h + 1) * GLA_DK)
    vs = slice(h * GLA_DV, (h + 1) * GLA_DV)
    nch = qi.shape[0] // c
    upd, dcol = [], []
    for n in range(nch):
        rs = slice(n * c, (n + 1) * c)
        upd.append(lax.dot_general(kend[rs, ks], vb[rs, vs], (((0,), (0,)), ((), ())), preferred_element_type=f32))
        d = jnp.transpose(jnp.broadcast_to(dec[n][:, ks], (GLA_DK, GLA_DK)))
        dcol.append(jnp.concatenate([d, d], axis=1))
    sh = jnp.where(reset, 0.0, S[h])
    outs = []
    for n in range(nch):
        rs = slice(n * c, (n + 1) * c)
        outs.append(_dot(qi[rs, ks], sh.astype(bf16)))
        sh = sh * dcol[n] + upd[n]
    S[h] = sh
    return jnp.concatenate(outs, axis=0)


def _interleave(order, **gens):
    def step(k):
        if k in gens:
            try:
                next(gens[k])
            except StopIteration:
                del gens[k]
    for k in order:
        step(k)
    while gens:
        for k in list(gens):
            step(k)


def _mix_prompt_kernel(xp_ref, xa_ref, vec_ref, wmain_ref, wlr_ref, wlr2_ref, rgbd_ref, wbr_ref, wbg_ref,
                       wo_ref, y_ref, h_ref, conv_ref, s_ref,
                       u_s, xrb, yr_s, ornn_s, hc, tail, S, hbuf, pre_s, *, bpb):
    j = pl.program_id(0)
    rb = GLA_BLOCK
    sw = 2 * RNN_BW
    nw = 512

    @pl.when(j == 0)
    def _():
        u_s[...] = jnp.zeros_like(u_s)
        xrb[...] = jnp.zeros_like(xrb)
        yr_s[...] = jnp.zeros_like(yr_s)
        ornn_s[...] = jnp.zeros_like(ornn_s)
        hc[...] = jnp.zeros_like(hc)
        tail[...] = jnp.zeros_like(tail)
        S[...] = jnp.zeros_like(S)

    p3 = j % 3
    p2 = j % 2
    r2 = (j + 1) % 2
    a3 = (j + 1) % 3
    r_first = (j + bpb - 1) % bpb == 0
    a_first = (j + 2 * bpb - 2) % bpb == 0

    def project():
        u = _rms(xp_ref[...], vec_ref[R_NORM + 2:R_NORM + 3, :]).astype(bf16)
        u_s[p3] = u
        yield
        for c0 in range(0, D_RNN, sw):
            xrb[p2, SUBLANES:SUBLANES + rb, c0:c0 + sw] = _dot(u, _w(wmain_ref[:, c0:c0 + sw]))
            yield
        for c0 in range(0, D_RNN, nw):
            yr_s[p2, :, c0:c0 + nw] = _dot(u, _w(wmain_ref[:, D_RNN + c0:D_RNN + c0 + nw]))
            yield

    def recur():
        xrb[r2, 0:SUBLANES, :] = jnp.where(r_first, 0.0, tail[...])
        rc = ROW_CHUNK
        sub = lax.broadcasted_iota(jnp.int32, (SUBLANES, sw), 0)
        scan_steps = [(sh, sub >= sh) for sh in (1, 2, 4)]
        first_row = jnp.logical_and(sub == 0, r_first)
        for s in range(D_RNN // sw):
            cs = slice(s * sw, (s + 1) * sw)
            for r0 in range(0, rb, rc):
                ext = xrb[r2, r0:r0 + SUBLANES + rc, cs]
                xc = vec_ref[R_CONVB:R_CONVB + 1, cs]
                for t in range(CONV_W):
                    lag = CONV_W - 1 - t
                    tap = ext if lag == 0 else pltpu.roll(ext, lag, 0)
                    xc = xc + tap[SUBLANES:, :] * vec_ref[R_CONVW + t:R_CONVW + t + 1, cs]
                hbuf[r0:r0 + rc, cs] = xc
            yield
            pre_s[s] = _dot(hbuf[:, cs].astype(bf16), _w(rgbd_ref[s]))
            yield
            hprev = jnp.where(r_first, 0.0, hc[:, cs])
            for r0 in range(0, rb, rc):
                hs = []
                for g0 in range(r0, r0 + rc, SUBLANES):
                    gs = slice(g0, g0 + SUBLANES)
                    xc = hbuf[gs, cs]
                    a, mult, gi = _rglru_coeffs(pre_s[s, gs, 0:sw], pre_s[s, gs, sw:2 * sw], vec_ref, cs)
                    if g0 == 0:
                        mult = jnp.where(first_row, 1.0, mult)
                    b = mult * gi * xc
                    for sh, m in scan_steps:
                        b = jnp.where(m, a * pltpu.roll(b, sh, 0) + b, b)
                        a = jnp.where(m, a * pltpu.roll(a, sh, 0), a)
                    hg = a * hprev + b
                    hs.append(hg)
                    hprev = jnp.broadcast_to(hg[SUBLANES - 1:SUBLANES, :], (SUBLANES, sw))
                h = jnp.concatenate(hs, axis=0)
                ornn_s[r2, r0:r0 + rc, cs] = (h * _gelu_tanh(yr_s[r2, r0:r0 + rc, cs])).astype(bf16)
                if (r0 // rc) % 2 == 1:
                    yield
            hc[:, cs] = hprev
        tail[...] = xrb[r2, rb:rb + SUBLANES, :]

    def attend():
        u = u_s[a3]
        o_rnn = ornn_s[p2]
        proj = []
        for c0 in range(OFF_Q, OFF_OG, nw):
            proj.append(_dot(u, _w(wmain_ref[:, c0:c0 + nw])))
            yield
        proj = jnp.concatenate(proj, axis=1)
        glog = _decay_log(u, wlr_ref, wlr2_ref, vec_ref)
        yield
        qi, ki, kend, vb, dec, mask = _gla_prepare(proj[:, 0:D_QK], proj[:, D_QK:2 * D_QK], proj[:, 2 * D_QK:], glog)
        yield
        og = []
        for c0 in range(OFF_OG, OFF_LR, nw):
            og.append(_dot(u, _w(wmain_ref[:, c0:c0 + nw])))
            yield
        og = jnp.concatenate(og, axis=1)
        gates = []
        o_heads = []
        gate_cols = iter(range(0, 2 * D_MODEL, nw))
        for h in range(GLA_HEADS):
            hs = slice(h * GLA_DV, (h + 1) * GLA_DV)
            oh = _gla_intra(qi, ki, vb, mask, h) + _gla_inter(qi, kend, vb, dec, S, h, a_first)
            o_heads.append((_head_norm(oh, vec_ref, h) * _silu(og[:, hs])).astype(bf16))
            yield
            for _ in range(2 * D_MODEL // nw // GLA_HEADS):
                c0 = next(gate_cols)
                gates.append(_dot(u, _w(wmain_ref[:, OFF_LR + c0:OFF_LR + c0 + nw])))
                yield
        part_a = []
        for i, c0 in enumerate(range(0, D_MODEL, nw)):
            part_a.append(_sigmoid(gates[i]) * _dot(o_rnn, _w(wbr_ref[:, c0:c0 + nw])))
            yield
        o_gla = jnp.concatenate(o_heads, axis=1)
        merged = []
        ng = D_MODEL // nw
        for i, c0 in enumerate(range(0, D_MODEL, nw)):
            yb = _dot(o_gla, _w(wbg_ref[:, c0:c0 + nw]))
            merged.append((part_a[i] + _sigmoid(gates[ng + i]) * yb).astype(bf16))
            yield
        merged = jnp.concatenate(merged, axis=1)
        mix = []
        for c0 in range(0, D_MODEL, nw):
            mix.append(_dot(merged, _w(wo_ref[:, c0:c0 + nw])))
            yield
        mix = jnp.concatenate(mix, axis=1)
        y_ref[...] = xa_ref[...] + _rms(mix, vec_ref[R_NORM + 3:R_NORM + 4, :])
        yield

    _interleave("", A=attend(), R=recur(), P=project())

    @pl.when((j + bpb - 1) % bpb == bpb - 1)
    def _():
        h_ref[...] = hc[0:1, :]
        conv_ref[...] = tail[SUBLANES - (CONV_W - 1):SUBLANES, :]

    @pl.when((j + 2 * bpb - 2) % bpb == bpb - 1)
    def _():
        s_ref[...] = S[...]


def _mix_prompt_call(x2d, vecs, wmain, wlr, wlr2, rgbd, wbr, wbg, wo, *, bsz, seq):
    rb = GLA_BLOCK
    bpb = seq // rb
    nblk = bsz * bpb
    consts = (vecs, wmain, wlr, wlr2, rgbd, wbr, wbg, wo)

    def blk(d):
        return lambda j: (jnp.clip(j - d, 0, nblk - 1), 0)

    def seq_of(d, nd):
        return lambda j: (jnp.clip(j - d, 0, nblk - 1) // bpb,) + (0,) * nd

    return pl.pallas_call(
        functools.partial(_mix_prompt_kernel, bpb=bpb),
        grid=(nblk + 2,),
        in_specs=[pl.BlockSpec((rb, D_MODEL), blk(0)), pl.BlockSpec((rb, D_MODEL), blk(2))]
        + [_const_spec(w.shape) for w in consts],
        out_specs=[pl.BlockSpec((rb, D_MODEL), blk(2)),
                   pl.BlockSpec((None, 1, D_RNN), seq_of(1, 2)),
                   pl.BlockSpec((None, CONV_W - 1, D_RNN), seq_of(1, 2)),
                   pl.BlockSpec((None, GLA_HEADS, GLA_DK, GLA_DV), seq_of(2, 3))],
        out_shape=[jax.ShapeDtypeStruct((nblk * rb, D_MODEL), f32),
                   jax.ShapeDtypeStruct((bsz, 1, D_RNN), f32),
                   jax.ShapeDtypeStruct((bsz, CONV_W - 1, D_RNN), f32),
                   jax.ShapeDtypeStruct((bsz, GLA_HEADS, GLA_DK, GLA_DV), f32)],
        scratch_shapes=[pltpu.VMEM((3, rb, D_MODEL), bf16),
                        pltpu.VMEM((2, rb + SUBLANES, D_RNN), f32),
                        pltpu.VMEM((2, rb, D_RNN), f32),
                        pltpu.VMEM((2, rb, D_RNN), bf16),
                        pltpu.VMEM((SUBLANES, D_RNN), f32),
                        pltpu.VMEM((SUBLANES, D_RNN), f32),
                        pltpu.VMEM((GLA_HEADS, GLA_DK, GLA_DV), f32),
                        pltpu.VMEM((rb, D_RNN), f32),
                        pltpu.VMEM((D_RNN // (2 * RNN_BW), rb, 4 * RNN_BW), f32)],
        compiler_params=pltpu.CompilerParams(dimension_semantics=("arbitrary",), vmem_limit_bytes=VMEM_LIMIT),
        name="mix_prompt",
    )(x2d, x2d, *consts)


def _sample_pre_kernel(x1_ref, h0_ref, c0_ref, vec_ref, wmain_ref, wlr_ref, wlr2_ref, rgbd_ref,
                       hn_ref, cn_ref, ornn_ref, qkvg_ref, glog_ref, gates_ref):
    u = _rms(x1_ref[...], vec_ref[R_NORM + 2:R_NORM + 3, :]).astype(bf16)
    xy = _dot(u, _w(wmain_ref[:, 0:2 * D_RNN]))
    xr = xy[:, :D_RNN]
    yr = xy[:, D_RNN:]
    xc = vec_ref[R_CONVB:R_CONVB + 1, :]
    for j in range(CONV_W - 1):
        xc = xc + c0_ref[j] * vec_ref[R_CONVW + j:R_CONVW + j + 1, :]
    xc = xc + xr * vec_ref[R_CONVW + CONV_W - 1:R_CONVW + CONV_W, :]
    for j in range(CONV_W - 2):
        cn_ref[j] = c0_ref[j + 1]
    cn_ref[CONV_W - 2] = xr
    ra, ri = _rglru_preact(xc, rgbd_ref)
    a, mult, gi = _rglru_coeffs(ra, ri, vec_ref)
    h = a * h0_ref[...] + mult * gi * xc
    hn_ref[...] = h
    ornn_ref[...] = (h * _gelu_tanh(yr)).astype(bf16)
    qkvg_ref[...] = _dot(u, _w(wmain_ref[:, OFF_Q:OFF_LR]))
    glog_ref[...] = _decay_log(u, wlr_ref, wlr2_ref, vec_ref)
    gates_ref[...] = _dot(u, _w(wmain_ref[:, OFF_LR:OFF_LR + 2 * D_MODEL]))


def _sample_pre_call(x1, h0, c0, vecs, wmain, wlr, wlr2, rgbd):
    n = x1.shape[0]
    args = (x1, h0, c0, vecs, wmain, wlr, wlr2, rgbd)
    outs = [((n, D_RNN), f32), ((CONV_W - 1, n, D_RNN), f32), ((n, D_RNN), bf16),
            ((n, OFF_LR - OFF_Q), f32), ((n, D_QK), f32), ((n, 2 * D_MODEL), f32)]
    return pl.pallas_call(
        _sample_pre_kernel,
        grid=(1,),
        in_specs=[_const_spec(a.shape) for a in args],
        out_specs=[pl.BlockSpec(s, lambda i, nd=len(s): (0,) * nd) for s, _ in outs],
        out_shape=[jax.ShapeDtypeStruct(s, d) for s, d in outs],
        compiler_params=pltpu.CompilerParams(dimension_semantics=("arbitrary",), vmem_limit_bytes=VMEM_LIMIT),
        name="sample_pre",
    )(*args)


def _col_bcast(rows8, width):
    tiled = jnp.concatenate([rows8] * (LANES // SUBLANES), axis=0)
    tt = jnp.transpose(tiled)
    return [jnp.broadcast_to(tt[:, j:j + 1], (LANES, width)) for j in range(SUBLANES)]


def _sample_gla_kernel(q_ref, k_ref, v_ref, g_ref, s0_ref, sn_ref, o_ref):
    scale = GLA_DK ** -0.5
    for h in range(GLA_HEADS):
        ks = slice(h * GLA_DK, (h + 1) * GLA_DK)
        vs = slice(h * GLA_DV, (h + 1) * GLA_DV)
        dcols = _col_bcast(jnp.exp(g_ref[:, ks]), GLA_DV)
        kcols = _col_bcast(k_ref[:, ks], GLA_DV)
        q8 = (q_ref[:, ks] * scale).astype(bf16)
        for j in range(SUBLANES):
            vrow = v_ref[j:j + 1, vs]
            sn = dcols[j] * s0_ref[j, h] + kcols[j] * vrow
            sn_ref[j, h] = sn
            o_ref[j:j + 1, vs] = _dot(q8, sn.astype(bf16))[j:j + 1, :]


def _sample_gla_call(q, k, v, g, s0):
    n = q.shape[0]
    sb = (SUBLANES, GLA_HEADS, GLA_DK, GLA_DV)
    return pl.pallas_call(
        _sample_gla_kernel,
        grid=(n // SUBLANES,),
        in_specs=[pl.BlockSpec((SUBLANES, D_QK), lambda i: (i, 0)),
                  pl.BlockSpec((SUBLANES, D_QK), lambda i: (i, 0)),
                  pl.BlockSpec((SUBLANES, D_V), lambda i: (i, 0)),
                  pl.BlockSpec((SUBLANES, D_QK), lambda i: (i, 0)),
                  pl.BlockSpec(sb, lambda i: (i, 0, 0, 0))],
        out_specs=[pl.BlockSpec(sb, lambda i: (i, 0, 0, 0)),
                   pl.BlockSpec((SUBLANES, D_V), lambda i: (i, 0))],
        out_shape=[jax.ShapeDtypeStruct(s0.shape, f32), jax.ShapeDtypeStruct((n, D_V), f32)],
        compiler_params=pltpu.CompilerParams(dimension_semantics=("arbitrary",), vmem_limit_bytes=VMEM_LIMIT),
        name="sample_gla",
    )(q, k, v, g, s0)


def _sample_post_kernel(x1_ref, ornn_ref, o_ref, og_ref, gates_ref, vec_ref, wbr_ref, wbg_ref, wo_ref, x2_ref):
    o = o_ref[...]
    og = og_ref[...]
    o_gla = jnp.concatenate(
        [_head_norm(o[:, h * GLA_DV:(h + 1) * GLA_DV], vec_ref, h) * _silu(og[:, h * GLA_DV:(h + 1) * GLA_DV])
         for h in range(GLA_HEADS)], axis=1).astype(bf16)
    gates = gates_ref[...]
    ya = _dot(ornn_ref[...], _w(wbr_ref[...]))
    yb = _dot(o_gla, _w(wbg_ref[...]))
    merged = _sigmoid(gates[:, :D_MODEL]) * ya + _sigmoid(gates[:, D_MODEL:]) * yb
    mix = _dot(merged.astype(bf16), _w(wo_ref[...]))
    x2_ref[...] = x1_ref[...] + _rms(mix, vec_ref[R_NORM + 3:R_NORM + 4, :])


def _sample_post_call(x1, ornn, o, og, gates, vecs, wbr, wbg, wo):
    n = x1.shape[0]
    args = (x1, ornn, o, og, gates, vecs, wbr, wbg, wo)
    return pl.pallas_call(
        _sample_post_kernel,
        grid=(1,),
        in_specs=[_const_spec(a.shape) for a in args],
        out_specs=pl.BlockSpec((n, D_MODEL), lambda i: (0, 0)),
        out_shape=jax.ShapeDtypeStruct((n, D_MODEL), f32),
        compiler_params=pltpu.CompilerParams(dimension_semantics=("arbitrary",), vmem_limit_bytes=VMEM_LIMIT),
        name="sample_post",
    )(*args)


def _pack_kernel(w_ref, o_ref):
    o_ref[...] = pltpu.bitcast(w_ref[...].astype(bf16), jnp.uint32)


def _pack(w, *, bn=None, name):
    nl, k, n = w.shape
    if bn is None:
        bn = PACK_COLS if n % PACK_COLS == 0 else n
    assert n % bn == 0
    return pl.pallas_call(
        _pack_kernel,
        grid=(nl, n // bn),
        in_specs=[pl.BlockSpec((None, k, bn), lambda l, j: (l, 0, j))],
        out_specs=pl.BlockSpec((None, k // 2, bn), lambda l, j: (l, 0, j)),
        out_shape=jax.ShapeDtypeStruct((nl, k // 2, n), jnp.uint32),
        compiler_params=pltpu.CompilerParams(dimension_semantics=("arbitrary", "arbitrary"),
                                             vmem_limit_bytes=VMEM_LIMIT),
        name=name,
    )(w)


def _pack_t_kernel(wt_ref, o_ref, *, keep):
    w = jnp.transpose(wt_ref[...])
    if keep < w.shape[1]:
        w = jnp.where(lax.broadcasted_iota(jnp.int32, w.shape, 1) < keep, w, 0.0)
    o_ref[...] = pltpu.bitcast(w.astype(bf16), jnp.uint32)


def _pack_t(wt, *, row0, nrows, keep=None, name):
    _, _, k = wt.shape
    bn = min(PACK_COLS, nrows)
    assert nrows % bn == 0 and row0 % SUBLANES == 0
    return pl.pallas_call(
        functools.partial(_pack_t_kernel, keep=nrows if keep is None else keep),
        grid=(nrows // bn,),
        in_specs=[pl.BlockSpec((None, pl.Element(bn), pl.Element(k)),
                               lambda j: (0, pl.multiple_of(row0 + j * bn, SUBLANES), 0))],
        out_specs=pl.BlockSpec((k // 2, bn), lambda j: (0, j)),
        out_shape=jax.ShapeDtypeStruct((k // 2, nrows), jnp.uint32),
        compiler_params=pltpu.CompilerParams(dimension_semantics=("arbitrary",), vmem_limit_bytes=VMEM_LIMIT),
        name=name,
    )(wt)


def _block_diag_gates(w_a, w_x):
    def bd(w):
        w = w.reshape(RNN_BLOCKS // 2, 2, RNN_BW, RNN_BW)
        z = jnp.zeros_like(w[:, 0])
        top = jnp.concatenate([w[:, 0], z], axis=2)
        bot = jnp.concatenate([z, w[:, 1]], axis=2)
        return jnp.concatenate([top, bot], axis=1)
    return jnp.concatenate([bd(w_a), bd(w_x)], axis=2)


def kernel(x_prompt, x_sample, state_rnn_h, state_rnn_conv, state_gla, norm_gains, ffn1_w_gu, ffn1_w_down, w_in, conv_w, conv_b, rg_w_a, rg_b_a, rg_w_x, rg_b_x, rg_lambda, gla_w_lr, gla_b_lr, gla_norm_g, w_branch_rnn, w_branch_gla, w_out, ffn2_w_gu, ffn2_w_down):
    assert w_in.shape == (1, D_MODEL, D_IN) and x_sample.shape[1] == 1
    bsz, seq, _ = x_prompt.shape
    nsmp = x_sample.shape[0]

    vecs = jnp.concatenate([
        norm_gains[0], conv_w[0], conv_b, rg_b_a, rg_b_x, rg_lambda,
        jnp.pad(gla_b_lr, ((0, 0), (0, D_MODEL - D_QK))), jnp.tile(gla_norm_g, (1, GLA_HEADS))], axis=0).astype(f32)
    w1gu = _pack(ffn1_w_gu, bn=D_FF // 2, name="pack_w1gu")[0]
    w1d = _pack(ffn1_w_down, name="pack_w1d")[0]
    w_in_t = jnp.swapaxes(w_in, 1, 2)
    wlr = _pack_t(w_in_t, row0=OFF_LR, nrows=LANES, keep=GLA_RANK, name="pack_wlr")
    wlr2 = _pack(jnp.pad(gla_w_lr, ((0, 0), (0, LANES - GLA_RANK), (0, 0))), name="pack_wlr2")[0]
    rgbd = _pack(_block_diag_gates(rg_w_a[0], rg_w_x[0]), name="pack_rgbd")

    xp = x_prompt.reshape(bsz * seq, D_MODEL)
    xs = x_sample.reshape(nsmp, D_MODEL)
    x1, x1s, w2gu, w2d, wmain, wbr, wbg, wo = _ffn_call(
        xp, xs, vecs, w1gu, w1d, row_pre=R_NORM, tm=512, name="ffn1",
        side=(("cols", ffn2_w_gu, 2 * LANES), ("rows", ffn2_w_down, LANES), ("t", w_in_t, 2 * LANES),
              ("cols", w_branch_rnn, 2 * LANES), ("cols", w_branch_gla, 2 * LANES), ("cols", w_out, 2 * LANES)))
    x2, hp, cp, sp = _mix_prompt_call(x1, vecs, wmain, wlr, wlr2, rgbd, wbr, wbg, wo, bsz=bsz, seq=seq)

    c0 = jnp.swapaxes(state_rnn_conv[0], 0, 1)
    hs, cs, ornn, qkvg, glog, gates = _sample_pre_call(x1s, state_rnn_h[0], c0, vecs, wmain, wlr, wlr2, rgbd)
    ss, osmp = _sample_gla_call(qkvg[:, 0:D_QK], qkvg[:, D_QK:2 * D_QK], qkvg[:, 2 * D_QK:2 * D_QK + D_V], glog,
                                state_gla[0])
    x2s = _sample_post_call(x1s, ornn, osmp, qkvg[:, 2 * D_QK + D_V:], gates, vecs, wbr, wbg, wo)

    yp, ys = _ffn_call(x2, x2s, vecs, w2gu, w2d, row_pre=R_NORM + 4, tm=512, name="ffn2")

    return (yp.reshape(bsz, seq, D_MODEL), ys.reshape(nsmp, 1, D_MODEL),
            hp.reshape(1, bsz, D_RNN), cp[None], sp[None],
            hs[None], jnp.swapaxes(cs, 0, 1)[None], ss[None])
```

```python
import functools
import math

import jax
import jax.numpy as jnp
from jax import lax
from jax.experimental import pallas as pl
from jax.experimental.pallas import tpu as pltpu

D_MODEL = 1024
D_RNN = 1024
RNN_BLOCKS = 8
RNN_BW = D_RNN // RNN_BLOCKS
CONV_W = 4
RG_C = 8.0
GLA_HEADS = 4
GLA_DK = 128
GLA_DV = 256
GLA_RANK = 16
GLA_TAU = 16.0
GLA_CHUNK = 64
D_FF = 2816
EPS = 1e-6

D_QK = GLA_HEADS * GLA_DK
D_V = GLA_HEADS * GLA_DV
OFF_Q = 2 * D_RNN
OFF_K = OFF_Q + D_QK
OFF_V = OFF_K + D_QK
OFF_OG = OFF_V + D_V
OFF_LR = OFF_OG + D_V
OFF_GATES = OFF_LR + GLA_RANK
D_IN = OFF_GATES + 2 * D_MODEL

LANES = 128
SUBLANES = 8
VMEM_LIMIT = 56 * 1024 * 1024
PACK_COLS = 512
SAMPLE_GLA_SEQS = 16
ROW_CHUNK = 32
GLA_BLOCK = 256

R_NORM = 0
R_CONVW = 6
R_CONVB = 10
R_BA = 11
R_BX = 12
R_LAM = 13
R_BLR = 14
R_GNORM = 15

bf16 = jnp.bfloat16
f32 = jnp.float32


def _w(packed):
    return pltpu.bitcast(packed, bf16)


def _dot(a, b):
    return jnp.dot(a, b, preferred_element_type=f32)


def _rms(x, g):
    return x * lax.rsqrt(jnp.mean(x * x, axis=-1, keepdims=True) + EPS) * g


def _sigmoid(x):
    return 0.5 * jnp.tanh(0.5 * x) + 0.5


def _silu(x):
    hx = 0.5 * x
    return hx * jnp.tanh(hx) + hx


def _softplus(x):
    return jnp.maximum(x, 0.0) + jnp.log1p(jnp.exp(-jnp.abs(x)))


def _sqrt_nonneg(m):
    return jnp.where(m > 0.0, m * lax.rsqrt(m), 0.0)


def _gelu_tanh(x):
    c = math.sqrt(2.0 / math.pi)
    hx = 0.5 * x
    return hx + hx * jnp.tanh(x * (c + (0.044715 * c) * (x * x)))


def _ffn(x, g_pre, g_post, wgu_ref, wd_ref):
    u = _rms(x, g_pre).astype(bf16)
    gu = _dot(u, _w(wgu_ref[...]))
    h = (_silu(gu[:, :D_FF]) * gu[:, D_FF:]).astype(bf16)
    y = _dot(h, _w(wd_ref[...]))
    return x + 0.5 * _rms(y, g_post)


def _rglru_preact(xc, rgbd_ref):
    xcb = xc.astype(bf16)
    slab = 2 * RNN_BW
    ra, ri = [], []
    for s in range(D_RNN // slab):
        gi = _dot(xcb[:, s * slab:(s + 1) * slab], _w(rgbd_ref[s]))
        ra.append(gi[:, :slab])
        ri.append(gi[:, slab:])
    return jnp.concatenate(ra, axis=1), jnp.concatenate(ri, axis=1)


def _rglru_coeffs(ra, ri, vec_ref, cols=slice(None)):
    i = _sigmoid(ri + vec_ref[R_BX:R_BX + 1, cols])
    half_scale = (-0.5 * RG_C) * _softplus(-vec_ref[R_LAM:R_LAM + 1, cols])
    log_a = half_scale * jnp.tanh(0.5 * (ra + vec_ref[R_BA:R_BA + 1, cols])) + half_scale
    a = jnp.exp(log_a)
    mult = _sqrt_nonneg(jnp.abs(jnp.tanh(log_a)) * (a * a + 1.0))
    return a, mult, i


def _head_norm(oh, vec_ref, h):
    gh = vec_ref[R_GNORM:R_GNORM + 1, h * GLA_DV:(h + 1) * GLA_DV]
    return oh * lax.rsqrt(jnp.mean(oh * oh, axis=-1, keepdims=True) + EPS) * gh


def _decay_log(u, wlr_ref, wlr2_ref, vec_ref):
    lr = _dot(u, _w(wlr_ref[...]))
    z = _dot(lr.astype(bf16), _w(wlr2_ref[...])) + vec_ref[R_BLR:R_BLR + 1, 0:D_QK]
    softplus_neg = jnp.maximum(-z, 0.0) + jnp.log(1.0 + jnp.exp(-jnp.abs(z)))
    return -softplus_neg * (1.0 / GLA_TAU)


def _ffn_kernel(xp_ref, xs_ref, vec_ref, wgu_ref, wd_ref, *rest, row_pre, nsteps, side_kinds):
    ns = len(side_kinds)
    side_in, (op_ref, os_ref), side_out = rest[:ns], rest[ns:ns + 2], rest[ns + 2:]
    i = pl.program_id(0)
    g_pre = vec_ref[row_pre:row_pre + 1, :]
    g_post = vec_ref[row_pre + 1:row_pre + 2, :]

    @pl.when(i < nsteps)
    def _():
        op_ref[...] = _ffn(xp_ref[...], g_pre, g_post, wgu_ref, wd_ref)
        for kind, src, dst in zip(side_kinds, side_in, side_out):
            w = jnp.transpose(src[...]) if kind == "t" else src[...]
            dst[...] = pltpu.bitcast(w.astype(bf16), jnp.uint32)

    @pl.when(i == nsteps)
    def _():
        os_ref[...] = _ffn(xs_ref[...], g_pre, g_post, wgu_ref, wd_ref)


def _const_spec(shape):
    nd = len(shape)
    return pl.BlockSpec(shape, lambda *_: (0,) * nd, pipeline_mode=pl.Buffered(1))


def _side_pack(kind, w, b, nsteps):
    if kind == "cols":
        _, k, n = w.shape
        nb = n // b
        blk = lambda i: jnp.minimum(i, nb - 1)
        specs = (pl.BlockSpec((None, k, b), lambda i: (0, 0, blk(i))), pl.BlockSpec((k // 2, b), lambda i: (0, blk(i))))
        shape = (k // 2, n)
    elif kind == "rows":
        _, k, n = w.shape
        nb = k // b
        blk = lambda i: jnp.minimum(i, nb - 1)
        specs = (pl.BlockSpec((None, b, n), lambda i: (0, blk(i), 0)), pl.BlockSpec((b // 2, n), lambda i: (blk(i), 0)))
        shape = (k // 2, n)
    else:
        _, _, k = w.shape
        nmain = OFF_LR // b
        nb = nmain + 2 * D_MODEL // b
        blk = lambda i: jnp.minimum(i, nb - 1)
        row = lambda i: pl.multiple_of(jnp.where(blk(i) < nmain, blk(i) * b, OFF_GATES + (blk(i) - nmain) * b), SUBLANES)
        specs = (pl.BlockSpec((None, pl.Element(b), pl.Element(k)), lambda i: (0, row(i), 0)),
                 pl.BlockSpec((k // 2, b), lambda i: (0, blk(i))))
        shape = (k // 2, nb * b)
    assert nb <= nsteps
    return specs, jax.ShapeDtypeStruct(shape, jnp.uint32)


def _ffn_call(xp, xs, vecs, wgu, wd, *, row_pre, tm, name, side=()):
    n = xp.shape[0]
    nsteps = n // tm
    tile = lambda i: (jnp.minimum(i, nsteps - 1), 0)
    packs = [_side_pack(kind, w, b, nsteps) for kind, w, b in side]
    return pl.pallas_call(
        functools.partial(_ffn_kernel, row_pre=row_pre, nsteps=nsteps, side_kinds=tuple(k for k, _, _ in side)),
        grid=(nsteps + 1,),
        in_specs=[pl.BlockSpec((tm, D_MODEL), tile), _const_spec(xs.shape),
                  _const_spec(vecs.shape), _const_spec(wgu.shape), _const_spec(wd.shape)]
        + [specs[0] for specs, _ in packs],
        out_specs=[pl.BlockSpec((tm, D_MODEL), tile), pl.BlockSpec(xs.shape, lambda i: (0, 0))]
        + [specs[1] for specs, _ in packs],
        out_shape=[jax.ShapeDtypeStruct((n, D_MODEL), f32), jax.ShapeDtypeStruct(xs.shape, f32)]
        + [shape for _, shape in packs],
        compiler_params=pltpu.CompilerParams(dimension_semantics=("arbitrary",), vmem_limit_bytes=VMEM_LIMIT),
        name=name,
    )(xp, xs, vecs, wgu, wd, *[w for _, w, _ in side])


def _gla_prepare(q, k, v, glog):
    rows = q.shape[0]
    c = GLA_CHUNK
    shift = c.bit_length() - 1
    ri = lax.broadcasted_iota(jnp.int32, (rows, rows), 0)
    ci = lax.broadcasted_iota(jnp.int32, (rows, rows), 1)
    mask = jnp.logical_and(lax.shift_right_logical(ri, shift) == lax.shift_right_logical(ci, shift), ri >= ci)
    ltri = mask.astype(bf16)
    g1 = glog.astype(bf16)
    r1 = glog - g1.astype(f32)
    g2 = r1.astype(bf16)
    g3 = (r1 - g2.astype(f32)).astype(bf16)
    bcum = _dot(ltri, g1) + _dot(ltri, g2) + _dot(ltri, g3)
    blast = [bcum[(n + 1) * c - 1:(n + 1) * c, :] for n in range(rows // c)]
    blast_b = jnp.concatenate([jnp.broadcast_to(b, (c, D_QK)) for b in blast], axis=0)
    qi = (q * (GLA_DK ** -0.5) * jnp.exp(bcum)).astype(bf16)
    ki = (k * jnp.exp(-bcum)).astype(bf16)
    kend = (k * jnp.exp(blast_b - bcum)).astype(bf16)
    dec = [jnp.exp(b) for b in blast]
    return qi, ki, kend, v.astype(bf16), dec, mask


def _gla_intra(qi, ki, vb, mask, h):
    ks = slice(h * GLA_DK, (h + 1) * GLA_DK)
    vs = slice(h * GLA_DV, (h + 1) * GLA_DV)
    att = lax.dot_general(qi[:, ks], ki[:, ks], (((1,), (1,)), ((), ())), preferred_element_type=f32)
    att = jnp.where(mask, att, 0.0).astype(bf16)
    return _dot(att, vb[:, vs])


def _gla_inter(qi, kend, vb, dec, S, h, reset):
    c = GLA_CHUNK
    ks = slice(h * GLA_DK, (h + 1) * GLA_DK)
    vs = slice(h * GLA_DV, (h + 1) * GLA_DV)
    nch = qi.shape[0] // c
    upd, dcol = [], []
    for n in range(nch):
        rs = slice(n * c, (n + 1) * c)
        upd.append(lax.dot_general(kend[rs, ks], vb[rs, vs], (((0,), (0,)), ((), ())), preferred_element_type=f32))
        d = jnp.transpose(jnp.broadcast_to(dec[n][:, ks], (GLA_DK, GLA_DK)))
        dcol.append(jnp.concatenate([d, d], axis=1))
    sh = jnp.where(reset, 0.0, S[h])
    outs = []
    for n in range(nch):
        rs = slice(n * c, (n + 1) * c)
        outs.append(_dot(qi[rs, ks], sh.astype(bf16)))
        sh = sh * dcol[n] + upd[n]
    S[h] = sh
    return jnp.concatenate(outs, axis=0)


def _interleave(order, **gens):
    def step(k):
        if k in gens:
            try:
                next(gens[k])
            except StopIteration:
                del gens[k]
    for k in order:
        step(k)
    while gens:
        for k in list(gens):
            step(k)


def _mix_prompt_kernel(xp_ref, xa_ref, vec_ref, wmain_ref, wlr_ref, wlr2_ref, rgbd_ref, wbr_ref, wbg_ref,
                       wo_ref, y_ref, h_ref, conv_ref, s_ref,
                       u_s, xrb, yr_s, ornn_s, hc, tail, S, hbuf, pre_s, *, bpb):
    j = pl.program_id(0)
    rb = GLA_BLOCK
    sw = 2 * RNN_BW
    nw = 512

    @pl.when(j == 0)
    def _():
        u_s[...] = jnp.zeros_like(u_s)
        xrb[...] = jnp.zeros_like(xrb)
        yr_s[...] = jnp.zeros_like(yr_s)
        ornn_s[...] = jnp.zeros_like(ornn_s)
        hc[...] = jnp.zeros_like(hc)
        tail[...] = jnp.zeros_like(tail)
        S[...] = jnp.zeros_like(S)

    p3 = j % 3
    p2 = j % 2
    r2 = (j + 1) % 2
    a3 = (j + 1) % 3
    r_first = (j + bpb - 1) % bpb == 0
    a_first = (j + 2 * bpb - 2) % bpb == 0

    def project():
        u = _rms(xp_ref[...], vec_ref[R_NORM + 2:R_NORM + 3, :]).astype(bf16)
        u_s[p3] = u
        yield
        for c0 in range(0, D_RNN, sw):
            xrb[p2, SUBLANES:SUBLANES + rb, c0:c0 + sw] = _dot(u, _w(wmain_ref[:, c0:c0 + sw]))
            yield
        for c0 in range(0, D_RNN, nw):
            yr_s[p2, :, c0:c0 + nw] = _dot(u, _w(wmain_ref[:, D_RNN + c0:D_RNN + c0 + nw]))
            yield

    def recur():
        xrb[r2, 0:SUBLANES, :] = jnp.where(r_first, 0.0, tail[...])
        rc = ROW_CHUNK
        sub = lax.broadcasted_iota(jnp.int32, (SUBLANES, sw), 0)
        scan_steps = [(sh, sub >= sh) for sh in (1, 2, 4)]
        first_row = jnp.logical_and(sub == 0, r_first)
        for s in range(D_RNN // sw):
            cs = slice(s * sw, (s + 1) * sw)
            for r0 in range(0, rb, rc):
                ext = xrb[r2, r0:r0 + SUBLANES + rc, cs]
                xc = vec_ref[R_CONVB:R_CONVB + 1, cs]
                for t in range(CONV_W):
                    lag = CONV_W - 1 - t
                    tap = ext if lag == 0 else pltpu.roll(ext, lag, 0)
                    xc = xc + tap[SUBLANES:, :] * vec_ref[R_CONVW + t:R_CONVW + t + 1, cs]
                hbuf[r0:r0 + rc, cs] = xc
            yield
            pre_s[s] = _dot(hbuf[:, cs].astype(bf16), _w(rgbd_ref[s]))
            yield
            hprev = jnp.where(r_first, 0.0, hc[:, cs])
            for r0 in range(0, rb, rc):
                hs = []
                for g0 in range(r0, r0 + rc, SUBLANES):
                    gs = slice(g0, g0 + SUBLANES)
                    xc = hbuf[gs, cs]
                    a, mult, gi = _rglru_coeffs(pre_s[s, gs, 0:sw], pre_s[s, gs, sw:2 * sw], vec_ref, cs)
                    if g0 == 0:
                        mult = jnp.where(first_row, 1.0, mult)
                    b = mult * gi * xc
                    for sh, m in scan_steps:
                        b = jnp.where(m, a * pltpu.roll(b, sh, 0) + b, b)
                        a = jnp.where(m, a * pltpu.roll(a, sh, 0), a)
                    hg = a * hprev + b
                    hs.append(hg)
                    hprev = jnp.broadcast_to(hg[SUBLANES - 1:SUBLANES, :], (SUBLANES, sw))
                h = jnp.concatenate(hs, axis=0)
                ornn_s[r2, r0:r0 + rc, cs] = (h * _gelu_tanh(yr_s[r2, r0:r0 + rc, cs])).astype(bf16)
                if (r0 // rc) % 2 == 1:
                    yield
            hc[:, cs] = hprev
        tail[...] = xrb[r2, rb:rb + SUBLANES, :]

    def attend():
        u = u_s[a3]
        o_rnn = ornn_s[p2]
        proj = []
        for c0 in range(OFF_Q, OFF_OG, nw):
            proj.append(_dot(u, _w(wmain_ref[:, c0:c0 + nw])))
            yield
        proj = jnp.concatenate(proj, axis=1)
        glog = _decay_log(u, wlr_ref, wlr2_ref, vec_ref)
        yield
        qi, ki, kend, vb, dec, mask = _gla_prepare(proj[:, 0:D_QK], proj[:, D_QK:2 * D_QK], proj[:, 2 * D_QK:], glog)
        yield
        og = []
        for c0 in range(OFF_OG, OFF_LR, nw):
            og.append(_dot(u, _w(wmain_ref[:, c0:c0 + nw])))
            yield
        og = jnp.concatenate(og, axis=1)
        gates = []
        o_heads = []
        gate_cols = iter(range(0, 2 * D_MODEL, nw))
        for h in range(GLA_HEADS):
            hs = slice(h * GLA_DV, (h + 1) * GLA_DV)
            oh = _gla_intra(qi, ki, vb, mask, h) + _gla_inter(qi, kend, vb, dec, S, h, a_first)
            o_heads.append((_head_norm(oh, vec_ref, h) * _silu(og[:, hs])).astype(bf16))
            yield
            for _ in range(2 * D_MODEL // nw // GLA_HEADS):
                c0 = next(gate_cols)
                gates.append(_dot(u, _w(wmain_ref[:, OFF_LR + c0:OFF_LR + c0 + nw])))
                yield
        part_a = []
        for i, c0 in enumerate(range(0, D_MODEL, nw)):
            part_a.append(_sigmoid(gates[i]) * _dot(o_rnn, _w(wbr_ref[:, c0:c0 + nw])))
            yield
        o_gla = jnp.concatenate(o_heads, axis=1)
        merged = []
        ng = D_MODEL // nw
        for i, c0 in enumerate(range(0, D_MODEL, nw)):
            yb = _dot(o_gla, _w(wbg_ref[:, c0:c0 + nw]))
            merged.append((part_a[i] + _sigmoid(gates[ng + i]) * yb).astype(bf16))
            yield
        merged = jnp.concatenate(merged, axis=1)
        mix = []
        for c0 in range(0, D_MODEL, nw):
            mix.append(_dot(merged, _w(wo_ref[:, c0:c0 + nw])))
            yield
        mix = jnp.concatenate(mix, axis=1)
        y_ref[...] = xa_ref[...] + _rms(mix, vec_ref[R_NORM + 3:R_NORM + 4, :])
        yield

    _interleave("", A=attend(), R=recur(), P=project())

    @pl.when((j + bpb - 1) % bpb == bpb - 1)
    def _():
        h_ref[...] = hc[0:1, :]
        conv_ref[...] = tail[SUBLANES - (CONV_W - 1):SUBLANES, :]

    @pl.when((j + 2 * bpb - 2) % bpb == bpb - 1)
    def _():
        s_ref[...] = S[...]


def _mix_prompt_call(x2d, vecs, wmain, wlr, wlr2, rgbd, wbr, wbg, wo, *, bsz, seq):
    rb = GLA_BLOCK
    bpb = seq // rb
    nblk = bsz * bpb
    consts = (vecs, wmain, wlr, wlr2, rgbd, wbr, wbg, wo)

    def blk(d):
        return lambda j: (jnp.clip(j - d, 0, nblk - 1), 0)

    def seq_of(d, nd):
        return lambda j: (jnp.clip(j - d, 0, nblk - 1) // bpb,) + (0,) * nd

    return pl.pallas_call(
        functools.partial(_mix_prompt_kernel, bpb=bpb),
        grid=(nblk + 2,),
        in_specs=[pl.BlockSpec((rb, D_MODEL), blk(0)), pl.BlockSpec((rb, D_MODEL), blk(2))]
        + [_const_spec(w.shape) for w in consts],
        out_specs=[pl.BlockSpec((rb, D_MODEL), blk(2)),
                   pl.BlockSpec((None, 1, D_RNN), seq_of(1, 2)),
                   pl.BlockSpec((None, CONV_W - 1, D_RNN), seq_of(1, 2)),
                   pl.BlockSpec((None, GLA_HEADS, GLA_DK, GLA_DV), seq_of(2, 3))],
        out_shape=[jax.ShapeDtypeStruct((nblk * rb, D_MODEL), f32),
                   jax.ShapeDtypeStruct((bsz, 1, D_RNN), f32),
                   jax.ShapeDtypeStruct((bsz, CONV_W - 1, D_RNN), f32),
                   jax.ShapeDtypeStruct((bsz, GLA_HEADS, GLA_DK, GLA_DV), f32)],
        scratch_shapes=[pltpu.VMEM((3, rb, D_MODEL), bf16),
                        pltpu.VMEM((2, rb + SUBLANES, D_RNN), f32),
                        pltpu.VMEM((2, rb, D_RNN), f32),
                        pltpu.VMEM((2, rb, D_RNN), bf16),
                        pltpu.VMEM((SUBLANES, D_RNN), f32),
                        pltpu.VMEM((SUBLANES, D_RNN), f32),
                        pltpu.VMEM((GLA_HEADS, GLA_DK, GLA_DV), f32),
                        pltpu.VMEM((rb, D_RNN), f32),
                        pltpu.VMEM((D_RNN // (2 * RNN_BW), rb, 4 * RNN_BW), f32)],
        compiler_params=pltpu.CompilerParams(dimension_semantics=("arbitrary",), vmem_limit_bytes=VMEM_LIMIT),
        name="mix_prompt",
    )(x2d, x2d, *consts)


def _sample_pre_kernel(x1_ref, h0_ref, c0_ref, vec_ref, wmain_ref, wlr_ref, wlr2_ref, rgbd_ref,
                       hn_ref, cn_ref, ornn_ref, qkvg_ref, glog_ref, gates_ref):
    u = _rms(x1_ref[...], vec_ref[R_NORM + 2:R_NORM + 3, :]).astype(bf16)
    xy = _dot(u, _w(wmain_ref[:, 0:2 * D_RNN]))
    xr = xy[:, :D_RNN]
    yr = xy[:, D_RNN:]
    xc = vec_ref[R_CONVB:R_CONVB + 1, :]
    for j in range(CONV_W - 1):
        xc = xc + c0_ref[j] * vec_ref[R_CONVW + j:R_CONVW + j + 1, :]
    xc = xc + xr * vec_ref[R_CONVW + CONV_W - 1:R_CONVW + CONV_W, :]
    for j in range(CONV_W - 2):
        cn_ref[j] = c0_ref[j + 1]
    cn_ref[CONV_W - 2] = xr
    ra, ri = _rglru_preact(xc, rgbd_ref)
    a, mult, gi = _rglru_coeffs(ra, ri, vec_ref)
    h = a * h0_ref[...] + mult * gi * xc
    hn_ref[...] = h
    ornn_ref[...] = (h * _gelu_tanh(yr)).astype(bf16)
    qkvg_ref[...] = _dot(u, _w(wmain_ref[:, OFF_Q:OFF_LR]))
    glog_ref[...] = _decay_log(u, wlr_ref, wlr2_ref, vec_ref)
    gates_ref[...] = _dot(u, _w(wmain_ref[:, OFF_LR:OFF_LR + 2 * D_MODEL]))


def _sample_pre_call(x1, h0, c0, vecs, wmain, wlr, wlr2, rgbd):
    n = x1.shape[0]
    args = (x1, h0, c0, vecs, wmain, wlr, wlr2, rgbd)
    outs = [((n, D_RNN), f32), ((CONV_W - 1, n, D_RNN), f32), ((n, D_RNN), bf16),
            ((n, OFF_LR - OFF_Q), f32), ((n, D_QK), f32), ((n, 2 * D_MODEL), f32)]
    return pl.pallas_call(
        _sample_pre_kernel,
        grid=(1,),
        in_specs=[_const_spec(a.shape) for a in args],
        out_specs=[pl.BlockSpec(s, lambda i, nd=len(s): (0,) * nd) for s, _ in outs],
        out_shape=[jax.ShapeDtypeStruct(s, d) for s, d in outs],
        compiler_params=pltpu.CompilerParams(dimension_semantics=("arbitrary",), vmem_limit_bytes=VMEM_LIMIT),
        name="sample_pre",
    )(*args)


def _col_bcast(rows8, width):
    tiled = jnp.concatenate([rows8] * (LANES // SUBLANES), axis=0)
    tt = jnp.transpose(tiled)
    return [jnp.broadcast_to(tt[:, j:j + 1], (LANES, width)) for j in range(SUBLANES)]


def _sample_gla_kernel(q_ref, k_ref, v_ref, g_ref, s0_ref, sn_ref, o_ref):
    scale = GLA_DK ** -0.5
    for r0 in range(0, q_ref.shape[0], SUBLANES):
        rows = slice(r0, r0 + SUBLANES)
        for h in range(GLA_HEADS):
            ks = slice(h * GLA_DK, (h + 1) * GLA_DK)
            vs = slice(h * GLA_DV, (h + 1) * GLA_DV)
            dcols = _col_bcast(jnp.exp(g_ref[rows, ks]), GLA_DV)
            kcols = _col_bcast(k_ref[rows, ks], GLA_DV)
            q8 = (q_ref[rows, ks] * scale).astype(bf16)
            for j in range(SUBLANES):
                vrow = v_ref[r0 + j:r0 + j + 1, vs]
                sn = dcols[j] * s0_ref[r0 + j, h] + kcols[j] * vrow
                sn_ref[r0 + j, h] = sn
                o_ref[r0 + j:r0 + j + 1, vs] = _dot(q8, sn.astype(bf16))[j:j + 1, :]


def _sample_gla_call(q, k, v, g, s0):
    n = q.shape[0]
    nb = SAMPLE_GLA_SEQS
    sb = (nb, GLA_HEADS, GLA_DK, GLA_DV)
    return pl.pallas_call(
        _sample_gla_kernel,
        grid=(n // nb,),
        in_specs=[pl.BlockSpec((nb, D_QK), lambda i: (i, 0)),
                  pl.BlockSpec((nb, D_QK), lambda i: (i, 0)),
                  pl.BlockSpec((nb, D_V), lambda i: (i, 0)),
                  pl.BlockSpec((nb, D_QK), lambda i: (i, 0)),
                  pl.BlockSpec(sb, lambda i: (i, 0, 0, 0))],
        out_specs=[pl.BlockSpec(sb, lambda i: (i, 0, 0, 0)),
                   pl.BlockSpec((nb, D_V), lambda i: (i, 0))],
        out_shape=[jax.ShapeDtypeStruct(s0.shape, f32), jax.ShapeDtypeStruct((n, D_V), f32)],
        compiler_params=pltpu.CompilerParams(dimension_semantics=("arbitrary",), vmem_limit_bytes=VMEM_LIMIT),
        name="sample_gla",
    )(q, k, v, g, s0)


def _sample_post_kernel(x1_ref, ornn_ref, o_ref, og_ref, gates_ref, vec_ref, wbr_ref, wbg_ref, wo_ref, x2_ref):
    o = o_ref[...]
    og = og_ref[...]
    o_gla = jnp.concatenate(
        [_head_norm(o[:, h * GLA_DV:(h + 1) * GLA_DV], vec_ref, h) * _silu(og[:, h * GLA_DV:(h + 1) * GLA_DV])
         for h in range(GLA_HEADS)], axis=1).astype(bf16)
    gates = gates_ref[...]
    ya = _dot(ornn_ref[...], _w(wbr_ref[...]))
    yb = _dot(o_gla, _w(wbg_ref[...]))
    merged = _sigmoid(gates[:, :D_MODEL]) * ya + _sigmoid(gates[:, D_MODEL:]) * yb
    mix = _dot(merged.astype(bf16), _w(wo_ref[...]))
    x2_ref[...] = x1_ref[...] + _rms(mix, vec_ref[R_NORM + 3:R_NORM + 4, :])


def _sample_post_call(x1, ornn, o, og, gates, vecs, wbr, wbg, wo):
    n = x1.shape[0]
    args = (x1, ornn, o, og, gates, vecs, wbr, wbg, wo)
    return pl.pallas_call(
        _sample_post_kernel,
        grid=(1,),
        in_specs=[_const_spec(a.shape) for a in args],
        out_specs=pl.BlockSpec((n, D_MODEL), lambda i: (0, 0)),
        out_shape=jax.ShapeDtypeStruct((n, D_MODEL), f32),
        compiler_params=pltpu.CompilerParams(dimension_semantics=("arbitrary",), vmem_limit_bytes=VMEM_LIMIT),
        name="sample_post",
    )(*args)


def _pack_kernel(w_ref, o_ref):
    o_ref[...] = pltpu.bitcast(w_ref[...].astype(bf16), jnp.uint32)


def _pack(w, *, bn=None, name):
    nl, k, n = w.shape
    if bn is None:
        bn = PACK_COLS if n % PACK_COLS == 0 else n
    assert n % bn == 0
    return pl.pallas_call(
        _pack_kernel,
        grid=(nl, n // bn),
        in_specs=[pl.BlockSpec((None, k, bn), lambda l, j: (l, 0, j))],
        out_specs=pl.BlockSpec((None, k // 2, bn), lambda l, j: (l, 0, j)),
        out_shape=jax.ShapeDtypeStruct((nl, k // 2, n), jnp.uint32),
        compiler_params=pltpu.CompilerParams(dimension_semantics=("arbitrary", "arbitrary"),
                                             vmem_limit_bytes=VMEM_LIMIT),
        name=name,
    )(w)


def _pack_t_kernel(wt_ref, o_ref, *, keep):
    w = jnp.transpose(wt_ref[...])
    if keep < w.shape[1]:
        w = jnp.where(lax.broadcasted_iota(jnp.int32, w.shape, 1) < keep, w, 0.0)
    o_ref[...] = pltpu.bitcast(w.astype(bf16), jnp.uint32)


def _pack_t(wt, *, row0, nrows, keep=None, name):
    _, _, k = wt.shape
    bn = min(PACK_COLS, nrows)
    assert nrows % bn == 0 and row0 % SUBLANES == 0
    return pl.pallas_call(
        functools.partial(_pack_t_kernel, keep=nrows if keep is None else keep),
        grid=(nrows // bn,),
        in_specs=[pl.BlockSpec((None, pl.Element(bn), pl.Element(k)),
                               lambda j: (0, pl.multiple_of(row0 + j * bn, SUBLANES), 0))],
        out_specs=pl.BlockSpec((k // 2, bn), lambda j: (0, j)),
        out_shape=jax.ShapeDtypeStruct((k // 2, nrows), jnp.uint32),
        compiler_params=pltpu.CompilerParams(dimension_semantics=("arbitrary",), vmem_limit_bytes=VMEM_LIMIT),
        name=name,
    )(wt)


def _block_diag_gates(w_a, w_x):
    def bd(w):
        w = w.reshape(RNN_BLOCKS // 2, 2, RNN_BW, RNN_BW)
        z = jnp.zeros_like(w[:, 0])
        top = jnp.concatenate([w[:, 0], z], axis=2)
        bot = jnp.concatenate([z, w[:, 1]], axis=2)
        return jnp.concatenate([top, bot], axis=1)
    return jnp.concatenate([bd(w_a), bd(w_x)], axis=2)


def kernel(x_prompt, x_sample, state_rnn_h, state_rnn_conv, state_gla, norm_gains, ffn1_w_gu, ffn1_w_down, w_in, conv_w, conv_b, rg_w_a, rg_b_a, rg_w_x, rg_b_x, rg_lambda, gla_w_lr, gla_b_lr, gla_norm_g, w_branch_rnn, w_branch_gla, w_out, ffn2_w_gu, ffn2_w_down):
    assert w_in.shape == (1, D_MODEL, D_IN) and x_sample.shape[1] == 1
    bsz, seq, _ = x_prompt.shape
    nsmp = x_sample.shape[0]

    vecs = jnp.concatenate([
        norm_gains[0], conv_w[0], conv_b, rg_b_a, rg_b_x, rg_lambda,
        jnp.pad(gla_b_lr, ((0, 0), (0, D_MODEL - D_QK))), jnp.tile(gla_norm_g, (1, GLA_HEADS))], axis=0).astype(f32)
    w1gu = _pack(ffn1_w_gu, bn=D_FF // 2, name="pack_w1gu")[0]
    w1d = _pack(ffn1_w_down, name="pack_w1d")[0]
    w_in_t = jnp.swapaxes(w_in, 1, 2)
    wlr = _pack_t(w_in_t, row0=OFF_LR, nrows=LANES, keep=GLA_RANK, name="pack_wlr")
    wlr2 = _pack(jnp.pad(gla_w_lr, ((0, 0), (0, LANES - GLA_RANK), (0, 0))), name="pack_wlr2")[0]
    rgbd = _pack(_block_diag_gates(rg_w_a[0], rg_w_x[0]), name="pack_rgbd")

    xp = x_prompt.reshape(bsz * seq, D_MODEL)
    xs = x_sample.reshape(nsmp, D_MODEL)
    x1, x1s, w2gu, w2d, wmain, wbr, wbg, wo = _ffn_call(
        xp, xs, vecs, w1gu, w1d, row_pre=R_NORM, tm=512, name="ffn1",
        side=(("cols", ffn2_w_gu, 2 * LANES), ("rows", ffn2_w_down, LANES), ("t", w_in_t, 2 * LANES),
              ("cols", w_branch_rnn, 2 * LANES), ("cols", w_branch_gla, 2 * LANES), ("cols", w_out, 2 * LANES)))
    x2, hp, cp, sp = _mix_prompt_call(x1, vecs, wmain, wlr, wlr2, rgbd, wbr, wbg, wo, bsz=bsz, seq=seq)

    c0 = jnp.swapaxes(state_rnn_conv[0], 0, 1)
    hs, cs, ornn, qkvg, glog, gates = _sample_pre_call(x1s, state_rnn_h[0], c0, vecs, wmain, wlr, wlr2, rgbd)
    ss, osmp = _sample_gla_call(qkvg[:, 0:D_QK], qkvg[:, D_QK:2 * D_QK], qkvg[:, 2 * D_QK:2 * D_QK + D_V], glog,
                                state_gla[0])
    x2s = _sample_post_call(x1s, ornn, osmp, qkvg[:, 2 * D_QK + D_V:], gates, vecs, wbr, wbg, wo)

    yp, ys = _ffn_call(x2, x2s, vecs, w2gu, w2d, row_pre=R_NORM + 4, tm=512, name="ffn2")

    return (yp.reshape(bsz, seq, D_MODEL), ys.reshape(nsmp, 1, D_MODEL),
            hp.reshape(1, bsz, D_RNN), cp[None], sp[None],
            hs[None], jnp.swapaxes(cs, 0, 1)[None], ss[None])
```

```python
import functools
import math

import jax
import jax.numpy as jnp
from jax import lax
from jax.experimental import pallas as pl
from jax.experimental.pallas import tpu as pltpu

D_MODEL = 1024
D_RNN = 1024
RNN_BLOCKS = 8
RNN_BW = D_RNN // RNN_BLOCKS
CONV_W = 4
RG_C = 8.0
GLA_HEADS = 4
GLA_DK = 128
GLA_DV = 256
GLA_RANK = 16
GLA_TAU = 16.0
GLA_CHUNK = 64
D_FF = 2816
EPS = 1e-6

D_QK = GLA_HEADS * GLA_DK
D_V = GLA_HEADS * GLA_DV
OFF_Q = 2 * D_RNN
OFF_K = OFF_Q + D_QK
OFF_V = OFF_K + D_QK
OFF_OG = OFF_V + D_V
OFF_LR = OFF_OG + D_V
OFF_GATES = OFF_LR + GLA_RANK
D_IN = OFF_GATES + 2 * D_MODEL

LANES = 128
SUBLANES = 8
VMEM_LIMIT = 56 * 1024 * 1024
PACK_COLS = 512
SAMPLE_GLA_SEQS = 16
ROW_CHUNK = 32
GLA_BLOCK = 256

R_NORM = 0
R_CONVW = 6
R_CONVB = 10
R_BA = 11
R_BX = 12
R_LAM = 13
R_BLR = 14
R_GNORM = 15

bf16 = jnp.bfloat16
f32 = jnp.float32


def _w(packed):
    return pltpu.bitcast(packed, bf16)


def _dot(a, b):
    return jnp.dot(a, b, preferred_element_type=f32)


def _rms(x, g):
    return x * lax.rsqrt(jnp.mean(x * x, axis=-1, keepdims=True) + EPS) * g


def _sigmoid(x):
    return 0.5 * jnp.tanh(0.5 * x) + 0.5


def _silu(x):
    hx = 0.5 * x
    return hx * jnp.tanh(hx) + hx


def _softplus(x):
    return jnp.maximum(x, 0.0) + jnp.log1p(jnp.exp(-jnp.abs(x)))


def _sqrt_nonneg(m):
    return jnp.where(m > 0.0, m * lax.rsqrt(m), 0.0)


def _gelu_tanh(x):
    c = math.sqrt(2.0 / math.pi)
    hx = 0.5 * x
    return hx + hx * jnp.tanh(x * (c + (0.044715 * c) * (x * x)))


def _ffn(x, g_pre, g_post, wgu_ref, wd_ref):
    n = x.shape[0]
    parts = [x] if n < 2 * GLA_BLOCK else [x[:n // 2], x[n // 2:]]
    us = [_rms(p, g_pre).astype(bf16) for p in parts]
    gus = [_dot(u, _w(wgu_ref[...])) for u in us]
    hs = [(_silu(gu[:, :D_FF]) * gu[:, D_FF:]).astype(bf16) for gu in gus]
    ys = [_dot(h, _w(wd_ref[...])) for h in hs]
    outs = [p + 0.5 * _rms(y, g_post) for p, y in zip(parts, ys)]
    return outs[0] if len(outs) == 1 else jnp.concatenate(outs, axis=0)


def _rglru_preact(xc, rgbd_ref):
    xcb = xc.astype(bf16)
    slab = 2 * RNN_BW
    ra, ri = [], []
    for s in range(D_RNN // slab):
        gi = _dot(xcb[:, s * slab:(s + 1) * slab], _w(rgbd_ref[s]))
        ra.append(gi[:, :slab])
        ri.append(gi[:, slab:])
    return jnp.concatenate(ra, axis=1), jnp.concatenate(ri, axis=1)


def _rglru_coeffs(ra, ri, vec_ref, cols=slice(None)):
    i = _sigmoid(ri + vec_ref[R_BX:R_BX + 1, cols])
    half_scale = (-0.5 * RG_C) * _softplus(-vec_ref[R_LAM:R_LAM + 1, cols])
    log_a = half_scale * jnp.tanh(0.5 * (ra + vec_ref[R_BA:R_BA + 1, cols])) + half_scale
    a = jnp.exp(log_a)
    mult = _sqrt_nonneg(jnp.abs(jnp.tanh(log_a)) * (a * a + 1.0))
    return a, mult, i


def _head_norm(oh, vec_ref, h):
    gh = vec_ref[R_GNORM:R_GNORM + 1, h * GLA_DV:(h + 1) * GLA_DV]
    return oh * lax.rsqrt(jnp.mean(oh * oh, axis=-1, keepdims=True) + EPS) * gh


def _decay_log(u, wlr_ref, wlr2_ref, vec_ref):
    lr = _dot(u, _w(wlr_ref[...]))
    z = _dot(lr.astype(bf16), _w(wlr2_ref[...])) + vec_ref[R_BLR:R_BLR + 1, 0:D_QK]
    softplus_neg = jnp.maximum(-z, 0.0) + jnp.log(1.0 + jnp.exp(-jnp.abs(z)))
    return -softplus_neg * (1.0 / GLA_TAU)


def _ffn_kernel(xp_ref, xs_ref, vec_ref, wgu_ref, wd_ref, *rest, row_pre, nsteps, side_kinds):
    ns = len(side_kinds)
    side_in, (op_ref, os_ref), side_out = rest[:ns], rest[ns:ns + 2], rest[ns + 2:]
    i = pl.program_id(0)
    g_pre = vec_ref[row_pre:row_pre + 1, :]
    g_post = vec_ref[row_pre + 1:row_pre + 2, :]

    @pl.when(i < nsteps)
    def _():
        op_ref[...] = _ffn(xp_ref[...], g_pre, g_post, wgu_ref, wd_ref)
        for kind, src, dst in zip(side_kinds, side_in, side_out):
            w = jnp.transpose(src[...]) if kind == "t" else src[...]
            dst[...] = pltpu.bitcast(w.astype(bf16), jnp.uint32)

    @pl.when(i == nsteps)
    def _():
        os_ref[...] = _ffn(xs_ref[...], g_pre, g_post, wgu_ref, wd_ref)


def _const_spec(shape):
    nd = len(shape)
    return pl.BlockSpec(shape, lambda *_: (0,) * nd, pipeline_mode=pl.Buffered(1))


def _side_pack(kind, w, b, nsteps):
    if kind == "cols":
        _, k, n = w.shape
        nb = n // b
        blk = lambda i: jnp.minimum(i, nb - 1)
        specs = (pl.BlockSpec((None, k, b), lambda i: (0, 0, blk(i))), pl.BlockSpec((k // 2, b), lambda i: (0, blk(i))))
        shape = (k // 2, n)
    elif kind == "rows":
        _, k, n = w.shape
        nb = k // b
        blk = lambda i: jnp.minimum(i, nb - 1)
        specs = (pl.BlockSpec((None, b, n), lambda i: (0, blk(i), 0)), pl.BlockSpec((b // 2, n), lambda i: (blk(i), 0)))
        shape = (k // 2, n)
    else:
        _, _, k = w.shape
        nmain = OFF_LR // b
        nb = nmain + 2 * D_MODEL // b
        blk = lambda i: jnp.minimum(i, nb - 1)
        row = lambda i: pl.multiple_of(jnp.where(blk(i) < nmain, blk(i) * b, OFF_GATES + (blk(i) - nmain) * b), SUBLANES)
        specs = (pl.BlockSpec((None, pl.Element(b), pl.Element(k)), lambda i: (0, row(i), 0)),
                 pl.BlockSpec((k // 2, b), lambda i: (0, blk(i))))
        shape = (k // 2, nb * b)
    assert nb <= nsteps
    return specs, jax.ShapeDtypeStruct(shape, jnp.uint32)


def _ffn_call(xp, xs, vecs, wgu, wd, *, row_pre, tm, name, side=()):
    n = xp.shape[0]
    nsteps = n // tm
    tile = lambda i: (jnp.minimum(i, nsteps - 1), 0)
    packs = [_side_pack(kind, w, b, nsteps) for kind, w, b in side]
    return pl.pallas_call(
        functools.partial(_ffn_kernel, row_pre=row_pre, nsteps=nsteps, side_kinds=tuple(k for k, _, _ in side)),
        grid=(nsteps + 1,),
        in_specs=[pl.BlockSpec((tm, D_MODEL), tile), _const_spec(xs.shape),
                  _const_spec(vecs.shape), _const_spec(wgu.shape), _const_spec(wd.shape)]
        + [specs[0] for specs, _ in packs],
        out_specs=[pl.BlockSpec((tm, D_MODEL), tile), pl.BlockSpec(xs.shape, lambda i: (0, 0))]
        + [specs[1] for specs, _ in packs],
        out_shape=[jax.ShapeDtypeStruct((n, D_MODEL), f32), jax.ShapeDtypeStruct(xs.shape, f32)]
        + [shape for _, shape in packs],
        compiler_params=pltpu.CompilerParams(dimension_semantics=("arbitrary",), vmem_limit_bytes=VMEM_LIMIT),
        name=name,
    )(xp, xs, vecs, wgu, wd, *[w for _, w, _ in side])


def _gla_prepare(q, k, v, glog):
    rows = q.shape[0]
    c = GLA_CHUNK
    shift = c.bit_length() - 1
    ri = lax.broadcasted_iota(jnp.int32, (rows, rows), 0)
    ci = lax.broadcasted_iota(jnp.int32, (rows, rows), 1)
    mask = jnp.logical_and(lax.shift_right_logical(ri, shift) == lax.shift_right_logical(ci, shift), ri >= ci)
    ltri = mask.astype(bf16)
    g1 = glog.astype(bf16)
    r1 = glog - g1.astype(f32)
    g2 = r1.astype(bf16)
    g3 = (r1 - g2.astype(f32)).astype(bf16)
    bcum = _dot(ltri, g1) + _dot(ltri, g2) + _dot(ltri, g3)
    blast = [bcum[(n + 1) * c - 1:(n + 1) * c, :] for n in range(rows // c)]
    blast_b = jnp.concatenate([jnp.broadcast_to(b, (c, D_QK)) for b in blast], axis=0)
    qi = (q * (GLA_DK ** -0.5) * jnp.exp(bcum)).astype(bf16)
    ki = (k * jnp.exp(-bcum)).astype(bf16)
    kend = (k * jnp.exp(blast_b - bcum)).astype(bf16)
    dec = [jnp.exp(b) for b in blast]
    return qi, ki, kend, v.astype(bf16), dec, mask


def _gla_intra(qi, ki, vb, mask, h):
    ks = slice(h * GLA_DK, (h + 1) * GLA_DK)
    vs = slice(h * GLA_DV, (h + 1) * GLA_DV)
    att = lax.dot_general(qi[:, ks], ki[:, ks], (((1,), (1,)), ((), ())), preferred_element_type=f32)
    att = jnp.where(mask, att, 0.0).astype(bf16)
    return _dot(att, vb[:, vs])


def _gla_inter(qi, kend, vb, dec, S, h, reset):
    c = GLA_CHUNK
    ks = slice(h * GLA_DK, (h + 1) * GLA_DK)
    vs = slice(h * GLA_DV, (h + 1) * GLA_DV)
    nch = qi.shape[0] // c
    upd, dcol = [], []
    for n in range(nch):
        rs = slice(n * c, (n + 1) * c)
        upd.append(lax.dot_general(kend[rs, ks], vb[rs, vs], (((0,), (0,)), ((), ())), preferred_element_type=f32))
        d = jnp.transpose(jnp.broadcast_to(dec[n][:, ks], (GLA_DK, GLA_DK)))
        dcol.append(jnp.concatenate([d, d], axis=1))
    sh = jnp.where(reset, 0.0, S[h])
    outs = []
    for n in range(nch):
        rs = slice(n * c, (n + 1) * c)
        outs.append(_dot(qi[rs, ks], sh.astype(bf16)))
        sh = sh * dcol[n] + upd[n]
    S[h] = sh
    return jnp.concatenate(outs, axis=0)


def _interleave(order, **gens):
    def step(k):
        if k in gens:
            try:
                next(gens[k])
            except StopIteration:
                del gens[k]
    for k in order:
        step(k)
    while gens:
        for k in list(gens):
            step(k)


def _mix_prompt_kernel(xp_ref, xa_ref, vec_ref, wmain_ref, wlr_ref, wlr2_ref, rgbd_ref, wbr_ref, wbg_ref,
                       wo_ref, y_ref, h_ref, conv_ref, s_ref,
                       u_s, xrb, yr_s, ornn_s, hc, tail, S, hbuf, pre_s, *, bpb):
    j = pl.program_id(0)
    rb = GLA_BLOCK
    sw = 2 * RNN_BW
    nw = 512

    @pl.when(j == 0)
    def _():
        u_s[...] = jnp.zeros_like(u_s)
        xrb[...] = jnp.zeros_like(xrb)
        yr_s[...] = jnp.zeros_like(yr_s)
        ornn_s[...] = jnp.zeros_like(ornn_s)
        hc[...] = jnp.zeros_like(hc)
        tail[...] = jnp.zeros_like(tail)
        S[...] = jnp.zeros_like(S)

    p3 = j % 3
    p2 = j % 2
    r2 = (j + 1) % 2
    a3 = (j + 1) % 3
    r_first = (j + bpb - 1) % bpb == 0
    a_first = (j + 2 * bpb - 2) % bpb == 0

    def project():
        u = _rms(xp_ref[...], vec_ref[R_NORM + 2:R_NORM + 3, :]).astype(bf16)
        u_s[p3] = u
        yield
        for c0 in range(0, D_RNN, sw):
            xrb[p2, SUBLANES:SUBLANES + rb, c0:c0 + sw] = _dot(u, _w(wmain_ref[:, c0:c0 + sw]))
            yield
        for c0 in range(0, D_RNN, nw):
            yr_s[p2, :, c0:c0 + nw] = _dot(u, _w(wmain_ref[:, D_RNN + c0:D_RNN + c0 + nw]))
            yield

    def recur():
        xrb[r2, 0:SUBLANES, :] = jnp.where(r_first, 0.0, tail[...])
        rc = ROW_CHUNK
        sub = lax.broadcasted_iota(jnp.int32, (SUBLANES, sw), 0)
        scan_steps = [(sh, sub >= sh) for sh in (1, 2, 4)]
        first_row = jnp.logical_and(sub == 0, r_first)
        for s in range(D_RNN // sw):
            cs = slice(s * sw, (s + 1) * sw)
            for r0 in range(0, rb, rc):
                ext = xrb[r2, r0:r0 + SUBLANES + rc, cs]
                xc = vec_ref[R_CONVB:R_CONVB + 1, cs]
                for t in range(CONV_W):
                    lag = CONV_W - 1 - t
                    tap = ext if lag == 0 else pltpu.roll(ext, lag, 0)
                    xc = xc + tap[SUBLANES:, :] * vec_ref[R_CONVW + t:R_CONVW + t + 1, cs]
                hbuf[r0:r0 + rc, cs] = xc
            yield
            pre_s[s] = _dot(hbuf[:, cs].astype(bf16), _w(rgbd_ref[s]))
            yield
            hprev = jnp.where(r_first, 0.0, hc[:, cs])
            for r0 in range(0, rb, rc):
                hs = []
                for g0 in range(r0, r0 + rc, SUBLANES):
                    gs = slice(g0, g0 + SUBLANES)
                    xc = hbuf[gs, cs]
                    a, mult, gi = _rglru_coeffs(pre_s[s, gs, 0:sw], pre_s[s, gs, sw:2 * sw], vec_ref, cs)
                    if g0 == 0:
                        mult = jnp.where(first_row, 1.0, mult)
                    b = mult * gi * xc
                    for sh, m in scan_steps:
                        b = jnp.where(m, a * pltpu.roll(b, sh, 0) + b, b)
                        a = jnp.where(m, a * pltpu.roll(a, sh, 0), a)
                    hg = a * hprev + b
                    hs.append(hg)
                    hprev = jnp.broadcast_to(hg[SUBLANES - 1:SUBLANES, :], (SUBLANES, sw))
                h = jnp.concatenate(hs, axis=0)
                ornn_s[r2, r0:r0 + rc, cs] = (h * _gelu_tanh(yr_s[r2, r0:r0 + rc, cs])).astype(bf16)
                if (r0 // rc) % 2 == 1:
                    yield
            hc[:, cs] = hprev
        tail[...] = xrb[r2, rb:rb + SUBLANES, :]

    def attend():
        u = u_s[a3]
        o_rnn = ornn_s[p2]
        proj = []
        for c0 in range(OFF_Q, OFF_OG, nw):
            proj.append(_dot(u, _w(wmain_ref[:, c0:c0 + nw])))
            yield
        proj = jnp.concatenate(proj, axis=1)
        glog = _decay_log(u, wlr_ref, wlr2_ref, vec_ref)
        yield
        qi, ki, kend, vb, dec, mask = _gla_prepare(proj[:, 0:D_QK], proj[:, D_QK:2 * D_QK], proj[:, 2 * D_QK:], glog)
        yield
        og = []
        for c0 in range(OFF_OG, OFF_LR, nw):
            og.append(_dot(u, _w(wmain_ref[:, c0:c0 + nw])))
            yield
        og = jnp.concatenate(og, axis=1)
        gates = []
        o_heads = []
        gate_cols = iter(range(0, 2 * D_MODEL, nw))
        for h in range(GLA_HEADS):
            hs = slice(h * GLA_DV, (h + 1) * GLA_DV)
            oh = _gla_intra(qi, ki, vb, mask, h) + _gla_inter(qi, kend, vb, dec, S, h, a_first)
            o_heads.append((_head_norm(oh, vec_ref, h) * _silu(og[:, hs])).astype(bf16))
            yield
            for _ in range(2 * D_MODEL // nw // GLA_HEADS):
                c0 = next(gate_cols)
                gates.append(_dot(u, _w(wmain_ref[:, OFF_LR + c0:OFF_LR + c0 + nw])))
                yield
        part_a = []
        for i, c0 in enumerate(range(0, D_MODEL, nw)):
            part_a.append(_sigmoid(gates[i]) * _dot(o_rnn, _w(wbr_ref[:, c0:c0 + nw])))
            yield
        o_gla = jnp.concatenate(o_heads, axis=1)
        merged = []
        ng = D_MODEL // nw
        for i, c0 in enumerate(range(0, D_MODEL, nw)):
            yb = _dot(o_gla, _w(wbg_ref[:, c0:c0 + nw]))
            merged.append((part_a[i] + _sigmoid(gates[ng + i]) * yb).astype(bf16))
            yield
        merged = jnp.concatenate(merged, axis=1)
        mix = []
        for c0 in range(0, D_MODEL, nw):
            mix.append(_dot(merged, _w(wo_ref[:, c0:c0 + nw])))
            yield
        mix = jnp.concatenate(mix, axis=1)
        y_ref[...] = xa_ref[...] + _rms(mix, vec_ref[R_NORM + 3:R_NORM + 4, :])
        yield

    _interleave("", A=attend(), R=recur(), P=project())

    @pl.when((j + bpb - 1) % bpb == bpb - 1)
    def _():
        h_ref[...] = hc[0:1, :]
        conv_ref[...] = tail[SUBLANES - (CONV_W - 1):SUBLANES, :]

    @pl.when((j + 2 * bpb - 2) % bpb == bpb - 1)
    def _():
        s_ref[...] = S[...]


def _mix_prompt_call(x2d, vecs, wmain, wlr, wlr2, rgbd, wbr, wbg, wo, *, bsz, seq):
    rb = GLA_BLOCK
    bpb = seq // rb
    nblk = bsz * bpb
    consts = (vecs, wmain, wlr, wlr2, rgbd, wbr, wbg, wo)

    def blk(d):
        return lambda j: (jnp.clip(j - d, 0, nblk - 1), 0)

    def seq_of(d, nd):
        return lambda j: (jnp.clip(j - d, 0, nblk - 1) // bpb,) + (0,) * nd

    return pl.pallas_call(
        functools.partial(_mix_prompt_kernel, bpb=bpb),
        grid=(nblk + 2,),
        in_specs=[pl.BlockSpec((rb, D_MODEL), blk(0)), pl.BlockSpec((rb, D_MODEL), blk(2))]
        + [_const_spec(w.shape) for w in consts],
        out_specs=[pl.BlockSpec((rb, D_MODEL), blk(2)),
                   pl.BlockSpec((None, 1, D_RNN), seq_of(1, 2)),
                   pl.BlockSpec((None, CONV_W - 1, D_RNN), seq_of(1, 2)),
                   pl.BlockSpec((None, GLA_HEADS, GLA_DK, GLA_DV), seq_of(2, 3))],
        out_shape=[jax.ShapeDtypeStruct((nblk * rb, D_MODEL), f32),
                   jax.ShapeDtypeStruct((bsz, 1, D_RNN), f32),
                   jax.ShapeDtypeStruct((bsz, CONV_W - 1, D_RNN), f32),
                   jax.ShapeDtypeStruct((bsz, GLA_HEADS, GLA_DK, GLA_DV), f32)],
        scratch_shapes=[pltpu.VMEM((3, rb, D_MODEL), bf16),
                        pltpu.VMEM((2, rb + SUBLANES, D_RNN), f32),
                        pltpu.VMEM((2, rb, D_RNN), f32),
                        pltpu.VMEM((2, rb, D_RNN), bf16),
                        pltpu.VMEM((SUBLANES, D_RNN), f32),
                        pltpu.VMEM((SUBLANES, D_RNN), f32),
                        pltpu.VMEM((GLA_HEADS, GLA_DK, GLA_DV), f32),
                        pltpu.VMEM((rb, D_RNN), f32),
                        pltpu.VMEM((D_RNN // (2 * RNN_BW), rb, 4 * RNN_BW), f32)],
        compiler_params=pltpu.CompilerParams(dimension_semantics=("arbitrary",), vmem_limit_bytes=VMEM_LIMIT),
        name="mix_prompt",
    )(x2d, x2d, *consts)


def _sample_pre_kernel(x1_ref, h0_ref, c0_ref, vec_ref, wmain_ref, wlr_ref, wlr2_ref, rgbd_ref,
                       hn_ref, cn_ref, ornn_ref, qkvg_ref, glog_ref, gates_ref):
    u = _rms(x1_ref[...], vec_ref[R_NORM + 2:R_NORM + 3, :]).astype(bf16)
    xy = _dot(u, _w(wmain_ref[:, 0:2 * D_RNN]))
    xr = xy[:, :D_RNN]
    yr = xy[:, D_RNN:]
    xc = vec_ref[R_CONVB:R_CONVB + 1, :]
    for j in range(CONV_W - 1):
        xc = xc + c0_ref[j] * vec_ref[R_CONVW + j:R_CONVW + j + 1, :]
    xc = xc + xr * vec_ref[R_CONVW + CONV_W - 1:R_CONVW + CONV_W, :]
    for j in range(CONV_W - 2):
        cn_ref[j] = c0_ref[j + 1]
    cn_ref[CONV_W - 2] = xr
    ra, ri = _rglru_preact(xc, rgbd_ref)
    a, mult, gi = _rglru_coeffs(ra, ri, vec_ref)
    h = a * h0_ref[...] + mult * gi * xc
    hn_ref[...] = h
    ornn_ref[...] = (h * _gelu_tanh(yr)).astype(bf16)
    qkvg_ref[...] = _dot(u, _w(wmain_ref[:, OFF_Q:OFF_LR]))
    glog_ref[...] = _decay_log(u, wlr_ref, wlr2_ref, vec_ref)
    gates_ref[...] = _dot(u, _w(wmain_ref[:, OFF_LR:OFF_LR + 2 * D_MODEL]))


def _sample_pre_call(x1, h0, c0, vecs, wmain, wlr, wlr2, rgbd):
    n = x1.shape[0]
    args = (x1, h0, c0, vecs, wmain, wlr, wlr2, rgbd)
    outs = [((n, D_RNN), f32), ((CONV_W - 1, n, D_RNN), f32), ((n, D_RNN), bf16),
            ((n, OFF_LR - OFF_Q), f32), ((n, D_QK), f32), ((n, 2 * D_MODEL), f32)]
    return pl.pallas_call(
        _sample_pre_kernel,
        grid=(1,),
        in_specs=[_const_spec(a.shape) for a in args],
        out_specs=[pl.BlockSpec(s, lambda i, nd=len(s): (0,) * nd) for s, _ in outs],
        out_shape=[jax.ShapeDtypeStruct(s, d) for s, d in outs],
        compiler_params=pltpu.CompilerParams(dimension_semantics=("arbitrary",), vmem_limit_bytes=VMEM_LIMIT),
        name="sample_pre",
    )(*args)


def _col_bcast(rows8, width):
    tiled = jnp.concatenate([rows8] * (LANES // SUBLANES), axis=0)
    tt = jnp.transpose(tiled)
    return [jnp.broadcast_to(tt[:, j:j + 1], (LANES, width)) for j in range(SUBLANES)]


def _sample_gla_update(q_ref, k_ref, v_ref, g_ref, s0_ref, sn_ref, o_ref):
    scale = GLA_DK ** -0.5
    ns = q_ref.shape[0]
    pad8 = lambda a: jnp.concatenate([a] * (SUBLANES // ns), axis=0)
    for h in range(GLA_HEADS):
        ks = slice(h * GLA_DK, (h + 1) * GLA_DK)
        vs = slice(h * GLA_DV, (h + 1) * GLA_DV)
        dcols = _col_bcast(pad8(jnp.exp(g_ref[:, ks])), GLA_DV)
        kcols = _col_bcast(pad8(k_ref[:, ks]), GLA_DV)
        q8 = pad8(q_ref[:, ks] * scale).astype(bf16)
        for j in range(ns):
            vrow = v_ref[j:j + 1, vs]
            sn = dcols[j] * s0_ref[j, h] + kcols[j] * vrow
            sn_ref[j, h] = sn
            o_ref[j:j + 1, vs] = _dot(q8, sn.astype(bf16))[j:j + 1, :]


def _ffn_gla_kernel(x_ref, vec_ref, wgu_ref, wd_ref, q_ref, k_ref, v_ref, g_ref, s0_ref, y_ref, sn_ref, o_ref, *,
                    row_pre):
    y_ref[...] = _ffn(x_ref[...], vec_ref[row_pre:row_pre + 1, :], vec_ref[row_pre + 1:row_pre + 2, :],
                      wgu_ref, wd_ref)
    _sample_gla_update(q_ref, k_ref, v_ref, g_ref, s0_ref, sn_ref, o_ref)


def _ffn_gla_call(x, vecs, wgu, wd, q, k, v, g, s0, *, row_pre, tm, name):
    n = x.shape[0]
    nsteps = n // tm
    nseq = q.shape[0]
    per = nseq // nsteps
    assert per * nsteps == nseq and SUBLANES % per == 0
    grp = lambda a: a.reshape(nsteps, per, a.shape[-1])
    row3 = lambda w: pl.BlockSpec((None, per, w), lambda i: (i, 0, 0))
    sb = pl.BlockSpec((per, GLA_HEADS, GLA_DK, GLA_DV), lambda i: (i, 0, 0, 0))
    y, sn, o = pl.pallas_call(
        functools.partial(_ffn_gla_kernel, row_pre=row_pre),
        grid=(nsteps,),
        in_specs=[pl.BlockSpec((tm, D_MODEL), lambda i: (i, 0)),
                  _const_spec(vecs.shape), _const_spec(wgu.shape), _const_spec(wd.shape),
                  row3(D_QK), row3(D_QK), row3(D_V), row3(D_QK), sb],
        out_specs=[pl.BlockSpec((tm, D_MODEL), lambda i: (i, 0)), sb, row3(D_V)],
        out_shape=[jax.ShapeDtypeStruct((n, D_MODEL), f32), jax.ShapeDtypeStruct(s0.shape, f32),
                   jax.ShapeDtypeStruct((nsteps, per, D_V), f32)],
        compiler_params=pltpu.CompilerParams(dimension_semantics=("arbitrary",), vmem_limit_bytes=VMEM_LIMIT),
        name=name,
    )(x, vecs, wgu, wd, grp(q), grp(k), grp(v), grp(g), s0)
    return y, sn, o.reshape(nseq, D_V)


def _sample_post_kernel(x1_ref, ornn_ref, o_ref, og_ref, gates_ref, vec_ref, wbr_ref, wbg_ref, wo_ref, wgu_ref, wd_ref,
                        y_ref):
    o = o_ref[...]
    og = og_ref[...]
    o_gla = jnp.concatenate(
        [_head_norm(o[:, h * GLA_DV:(h + 1) * GLA_DV], vec_ref, h) * _silu(og[:, h * GLA_DV:(h + 1) * GLA_DV])
         for h in range(GLA_HEADS)], axis=1).astype(bf16)
    gates = gates_ref[...]
    ya = _dot(ornn_ref[...], _w(wbr_ref[...]))
    yb = _dot(o_gla, _w(wbg_ref[...]))
    merged = _sigmoid(gates[:, :D_MODEL]) * ya + _sigmoid(gates[:, D_MODEL:]) * yb
    mix = _dot(merged.astype(bf16), _w(wo_ref[...]))
    x2 = x1_ref[...] + _rms(mix, vec_ref[R_NORM + 3:R_NORM + 4, :])
    y_ref[...] = _ffn(x2, vec_ref[R_NORM + 4:R_NORM + 5, :], vec_ref[R_NORM + 5:R_NORM + 6, :], wgu_ref, wd_ref)


def _sample_post_call(x1, ornn, o, og, gates, vecs, wbr, wbg, wo, wgu, wd):
    n = x1.shape[0]
    args = (x1, ornn, o, og, gates, vecs, wbr, wbg, wo, wgu, wd)
    return pl.pallas_call(
        _sample_post_kernel,
        grid=(1,),
        in_specs=[_const_spec(a.shape) for a in args],
        out_specs=pl.BlockSpec((n, D_MODEL), lambda i: (0, 0)),
        out_shape=jax.ShapeDtypeStruct((n, D_MODEL), f32),
        compiler_params=pltpu.CompilerParams(dimension_semantics=("arbitrary",), vmem_limit_bytes=VMEM_LIMIT),
        name="sample_post",
    )(*args)


def _pack_kernel(w_ref, o_ref):
    o_ref[...] = pltpu.bitcast(w_ref[...].astype(bf16), jnp.uint32)


def _pack(w, *, bn=None, name):
    nl, k, n = w.shape
    if bn is None:
        bn = PACK_COLS if n % PACK_COLS == 0 else n
    assert n % bn == 0
    return pl.pallas_call(
        _pack_kernel,
        grid=(nl, n // bn),
        in_specs=[pl.BlockSpec((None, k, bn), lambda l, j: (l, 0, j))],
        out_specs=pl.BlockSpec((None, k // 2, bn), lambda l, j: (l, 0, j)),
        out_shape=jax.ShapeDtypeStruct((nl, k // 2, n), jnp.uint32),
        compiler_params=pltpu.CompilerParams(dimension_semantics=("arbitrary", "arbitrary"),
                                             vmem_limit_bytes=VMEM_LIMIT),
        name=name,
    )(w)


def _pack_t_kernel(wt_ref, o_ref, *, keep):
    w = jnp.transpose(wt_ref[...])
    if keep < w.shape[1]:
        w = jnp.where(lax.broadcasted_iota(jnp.int32, w.shape, 1) < keep, w, 0.0)
    o_ref[...] = pltpu.bitcast(w.astype(bf16), jnp.uint32)


def _pack_t(wt, *, row0, nrows, keep=None, name):
    _, _, k = wt.shape
    bn = min(PACK_COLS, nrows)
    assert nrows % bn == 0 and row0 % SUBLANES == 0
    return pl.pallas_call(
        functools.partial(_pack_t_kernel, keep=nrows if keep is None else keep),
        grid=(nrows // bn,),
        in_specs=[pl.BlockSpec((None, pl.Element(bn), pl.Element(k)),
                               lambda j: (0, pl.multiple_of(row0 + j * bn, SUBLANES), 0))],
        out_specs=pl.BlockSpec((k // 2, bn), lambda j: (0, j)),
        out_shape=jax.ShapeDtypeStruct((k // 2, nrows), jnp.uint32),
        compiler_params=pltpu.CompilerParams(dimension_semantics=("arbitrary",), vmem_limit_bytes=VMEM_LIMIT),
        name=name,
    )(wt)


def _block_diag_gates(w_a, w_x):
    def bd(w):
        w = w.reshape(RNN_BLOCKS // 2, 2, RNN_BW, RNN_BW)
        z = jnp.zeros_like(w[:, 0])
        top = jnp.concatenate([w[:, 0], z], axis=2)
        bot = jnp.concatenate([z, w[:, 1]], axis=2)
        return jnp.concatenate([top, bot], axis=1)
    return jnp.concatenate([bd(w_a), bd(w_x)], axis=2)


def kernel(x_prompt, x_sample, state_rnn_h, state_rnn_conv, state_gla, norm_gains, ffn1_w_gu, ffn1_w_down, w_in, conv_w, conv_b, rg_w_a, rg_b_a, rg_w_x, rg_b_x, rg_lambda, gla_w_lr, gla_b_lr, gla_norm_g, w_branch_rnn, w_branch_gla, w_out, ffn2_w_gu, ffn2_w_down):
    assert w_in.shape == (1, D_MODEL, D_IN) and x_sample.shape[1] == 1
    bsz, seq, _ = x_prompt.shape
    nsmp = x_sample.shape[0]

    vecs = jnp.concatenate([
        norm_gains[0], conv_w[0], conv_b, rg_b_a, rg_b_x, rg_lambda,
        jnp.pad(gla_b_lr, ((0, 0), (0, D_MODEL - D_QK))), jnp.tile(gla_norm_g, (1, GLA_HEADS))], axis=0).astype(f32)
    w1gu = _pack(ffn1_w_gu, bn=D_FF // 2, name="pack_w1gu")[0]
    w1d = _pack(ffn1_w_down, name="pack_w1d")[0]
    w_in_t = jnp.swapaxes(w_in, 1, 2)
    wlr = _pack_t(w_in_t, row0=OFF_LR, nrows=LANES, keep=GLA_RANK, name="pack_wlr")
    wlr2 = _pack(jnp.pad(gla_w_lr, ((0, 0), (0, LANES - GLA_RANK), (0, 0))), name="pack_wlr2")[0]
    rgbd = _pack(_block_diag_gates(rg_w_a[0], rg_w_x[0]), name="pack_rgbd")

    xp = x_prompt.reshape(bsz * seq, D_MODEL)
    xs = x_sample.reshape(nsmp, D_MODEL)
    x1, x1s, w2gu, w2d, wmain, wbr, wbg, wo = _ffn_call(
        xp, xs, vecs, w1gu, w1d, row_pre=R_NORM, tm=512, name="ffn1",
        side=(("cols", ffn2_w_gu, 2 * LANES), ("rows", ffn2_w_down, LANES), ("t", w_in_t, 2 * LANES),
              ("cols", w_branch_rnn, 2 * LANES), ("cols", w_branch_gla, 2 * LANES), ("cols", w_out, 2 * LANES)))
    x2, hp, cp, sp = _mix_prompt_call(x1, vecs, wmain, wlr, wlr2, rgbd, wbr, wbg, wo, bsz=bsz, seq=seq)

    c0 = jnp.swapaxes(state_rnn_conv[0], 0, 1)
    hs, cs, ornn, qkvg, glog, gates = _sample_pre_call(x1s, state_rnn_h[0], c0, vecs, wmain, wlr, wlr2, rgbd)

    yp, ss, osmp = _ffn_gla_call(x2, vecs, w2gu, w2d, qkvg[:, 0:D_QK], qkvg[:, D_QK:2 * D_QK],
                                 qkvg[:, 2 * D_QK:2 * D_QK + D_V], glog, state_gla[0],
                                 row_pre=R_NORM + 4, tm=512, name="ffn2_gla")
    ys = _sample_post_call(x1s, ornn, osmp, qkvg[:, 2 * D_QK + D_V:], gates, vecs, wbr, wbg, wo, w2gu, w2d)

    return (yp.reshape(bsz, seq, D_MODEL), ys.reshape(nsmp, 1, D_MODEL),
            hp.reshape(1, bsz, D_RNN), cp[None], sp[None],
            hs[None], jnp.swapaxes(cs, 0, 1)[None], ss[None])
```

```python
import functools
import math

import jax
import jax.numpy as jnp
from jax import lax
from jax.experimental import pallas as pl
from jax.experimental.pallas import tpu as pltpu

D_MODEL = 1024
D_RNN = 1024
RNN_BLOCKS = 8
RNN_BW = D_RNN // RNN_BLOCKS
CONV_W = 4
RG_C = 8.0
GLA_HEADS = 4
GLA_DK = 128
GLA_DV = 256
GLA_RANK = 16
GLA_TAU = 16.0
GLA_CHUNK = 64
D_FF = 2816
EPS = 1e-6

D_QK = GLA_HEADS * GLA_DK
D_V = GLA_HEADS * GLA_DV
OFF_Q = 2 * D_RNN
OFF_K = OFF_Q + D_QK
OFF_V = OFF_K + D_QK
OFF_OG = OFF_V + D_V
OFF_LR = OFF_OG + D_V
OFF_GATES = OFF_LR + GLA_RANK
D_IN = OFF_GATES + 2 * D_MODEL

LANES = 128
SUBLANES = 8
VMEM_LIMIT = 56 * 1024 * 1024
PACK_COLS = 512
ROW_CHUNK = 32
GLA_BLOCK = 256

R_NORM = 0
R_CONVW = 6
R_CONVB = 10
R_BA = 11
R_BX = 12
R_LAM = 13
R_BLR = 14
R_GNORM = 15

bf16 = jnp.bfloat16
f32 = jnp.float32


def _w(packed):
    return pltpu.bitcast(packed, bf16)


def _dot(a, b):
    return jnp.dot(a, b, preferred_element_type=f32)


def _rms(x, g):
    return x * lax.rsqrt(jnp.mean(x * x, axis=-1, keepdims=True) + EPS) * g


def _sigmoid(x):
    return 0.5 * jnp.tanh(0.5 * x) + 0.5


def _silu(x):
    hx = 0.5 * x
    return hx * jnp.tanh(hx) + hx


def _softplus(x):
    return jnp.maximum(x, 0.0) + jnp.log1p(jnp.exp(-jnp.abs(x)))


def _sqrt_nonneg(m):
    return jnp.where(m > 0.0, m * lax.rsqrt(m), 0.0)


def _gelu_tanh(x):
    c = math.sqrt(2.0 / math.pi)
    hx = 0.5 * x
    return hx + hx * jnp.tanh(x * (c + (0.044715 * c) * (x * x)))


def _ffn(x, g_pre, g_post, wgu_ref, wd_ref):
    n = x.shape[0]
    parts = [x] if n < 2 * GLA_BLOCK else [x[:n // 2], x[n // 2:]]
    us = [_rms(p, g_pre).astype(bf16) for p in parts]
    gus = [_dot(u, _w(wgu_ref[...])) for u in us]
    hs = [(_silu(gu[:, :D_FF]) * gu[:, D_FF:]).astype(bf16) for gu in gus]
    ys = [_dot(h, _w(wd_ref[...])) for h in hs]
    outs = [p + 0.5 * _rms(y, g_post) for p, y in zip(parts, ys)]
    return outs[0] if len(outs) == 1 else jnp.concatenate(outs, axis=0)


def _rglru_preact(xc, rgbd_ref):
    xcb = xc.astype(bf16)
    slab = 2 * RNN_BW
    ra, ri = [], []
    for s in range(D_RNN // slab):
        gi = _dot(xcb[:, s * slab:(s + 1) * slab], _w(rgbd_ref[s]))
        ra.append(gi[:, :slab])
        ri.append(gi[:, slab:])
    return jnp.concatenate(ra, axis=1), jnp.concatenate(ri, axis=1)


def _rglru_coeffs(ra, ri, vec_ref, cols=slice(None)):
    i = _sigmoid(ri + vec_ref[R_BX:R_BX + 1, cols])
    half_scale = (-0.5 * RG_C) * _softplus(-vec_ref[R_LAM:R_LAM + 1, cols])
    log_a = half_scale * jnp.tanh(0.5 * (ra + vec_ref[R_BA:R_BA + 1, cols])) + half_scale
    a = jnp.exp(log_a)
    mult = _sqrt_nonneg(jnp.abs(jnp.tanh(log_a)) * (a * a + 1.0))
    return a, mult, i


def _head_norm(oh, vec_ref, h):
    gh = vec_ref[R_GNORM:R_GNORM + 1, h * GLA_DV:(h + 1) * GLA_DV]
    return oh * lax.rsqrt(jnp.mean(oh * oh, axis=-1, keepdims=True) + EPS) * gh


def _decay_log(u, wlr_ref, wlr2_ref, vec_ref):
    lr = _dot(u, _w(wlr_ref[...]))
    z = _dot(lr.astype(bf16), _w(wlr2_ref[...])) + vec_ref[R_BLR:R_BLR + 1, 0:D_QK]
    softplus_neg = jnp.maximum(-z, 0.0) + jnp.log(1.0 + jnp.exp(-jnp.abs(z)))
    return -softplus_neg * (1.0 / GLA_TAU)


def _ffn_kernel(xp_ref, xs_ref, vec_ref, wgu_ref, wd_ref, *rest, row_pre, nsteps, side_kinds):
    ns = len(side_kinds)
    side_in, (op_ref, os_ref), side_out = rest[:ns], rest[ns:ns + 2], rest[ns + 2:]
    i = pl.program_id(0)
    g_pre = vec_ref[row_pre:row_pre + 1, :]
    g_post = vec_ref[row_pre + 1:row_pre + 2, :]

    @pl.when(i < nsteps)
    def _():
        op_ref[...] = _ffn(xp_ref[...], g_pre, g_post, wgu_ref, wd_ref)
        for kind, src, dst in zip(side_kinds, side_in, side_out):
            w = jnp.transpose(src[...]) if kind == "t" else src[...]
            dst[...] = pltpu.bitcast(w.astype(bf16), jnp.uint32)

    @pl.when(i == nsteps)
    def _():
        os_ref[...] = _ffn(xs_ref[...], g_pre, g_post, wgu_ref, wd_ref)


def _const_spec(shape):
    nd = len(shape)
    return pl.BlockSpec(shape, lambda *_: (0,) * nd, pipeline_mode=pl.Buffered(1))


def _side_pack(kind, w, b, nsteps):
    if kind == "cols":
        _, k, n = w.shape
        nb = n // b
        blk = lambda i: jnp.minimum(i, nb - 1)
        specs = (pl.BlockSpec((None, k, b), lambda i: (0, 0, blk(i))), pl.BlockSpec((k // 2, b), lambda i: (0, blk(i))))
        shape = (k // 2, n)
    elif kind == "rows":
        _, k, n = w.shape
        nb = k // b
        blk = lambda i: jnp.minimum(i, nb - 1)
        specs = (pl.BlockSpec((None, b, n), lambda i: (0, blk(i), 0)), pl.BlockSpec((b // 2, n), lambda i: (blk(i), 0)))
        shape = (k // 2, n)
    else:
        _, _, k = w.shape
        nmain = OFF_LR // b
        nb = nmain + 2 * D_MODEL // b
        blk = lambda i: jnp.minimum(i, nb - 1)
        row = lambda i: pl.multiple_of(jnp.where(blk(i) < nmain, blk(i) * b, OFF_GATES + (blk(i) - nmain) * b), SUBLANES)
        specs = (pl.BlockSpec((None, pl.Element(b), pl.Element(k)), lambda i: (0, row(i), 0)),
                 pl.BlockSpec((k // 2, b), lambda i: (0, blk(i))))
        shape = (k // 2, nb * b)
    assert nb <= nsteps
    return specs, jax.ShapeDtypeStruct(shape, jnp.uint32)


def _ffn_call(xp, xs, vecs, wgu, wd, *, row_pre, tm, name, side=()):
    n = xp.shape[0]
    nsteps = n // tm
    tile = lambda i: (jnp.minimum(i, nsteps - 1), 0)
    packs = [_side_pack(kind, w, b, nsteps) for kind, w, b in side]
    return pl.pallas_call(
        functools.partial(_ffn_kernel, row_pre=row_pre, nsteps=nsteps, side_kinds=tuple(k for k, _, _ in side)),
        grid=(nsteps + 1,),
        in_specs=[pl.BlockSpec((tm, D_MODEL), tile), _const_spec(xs.shape),
                  _const_spec(vecs.shape), _const_spec(wgu.shape), _const_spec(wd.shape)]
        + [specs[0] for specs, _ in packs],
        out_specs=[pl.BlockSpec((tm, D_MODEL), tile), pl.BlockSpec(xs.shape, lambda i: (0, 0))]
        + [specs[1] for specs, _ in packs],
        out_shape=[jax.ShapeDtypeStruct((n, D_MODEL), f32), jax.ShapeDtypeStruct(xs.shape, f32)]
        + [shape for _, shape in packs],
        compiler_params=pltpu.CompilerParams(dimension_semantics=("arbitrary",), vmem_limit_bytes=VMEM_LIMIT),
        name=name,
    )(xp, xs, vecs, wgu, wd, *[w for _, w, _ in side])


def _gla_prepare(q, k, v, glog):
    rows = q.shape[0]
    c = GLA_CHUNK
    shift = c.bit_length() - 1
    ri = lax.broadcasted_iota(jnp.int32, (rows, rows), 0)
    ci = lax.broadcasted_iota(jnp.int32, (rows, rows), 1)
    mask = jnp.logical_and(lax.shift_right_logical(ri, shift) == lax.shift_right_logical(ci, shift), ri >= ci)
    ltri = mask.astype(bf16)
    g1 = glog.astype(bf16)
    r1 = glog - g1.astype(f32)
    g2 = r1.astype(bf16)
    g3 = (r1 - g2.astype(f32)).astype(bf16)
    bcum = _dot(ltri, g1) + _dot(ltri, g2) + _dot(ltri, g3)
    blast = [bcum[(n + 1) * c - 1:(n + 1) * c, :] for n in range(rows // c)]
    blast_b = jnp.concatenate([jnp.broadcast_to(b, (c, D_QK)) for b in blast], axis=0)
    qi = (q * (GLA_DK ** -0.5) * jnp.exp(bcum)).astype(bf16)
    ki = (k * jnp.exp(-bcum)).astype(bf16)
    kend = (k * jnp.exp(blast_b - bcum)).astype(bf16)
    dec = [jnp.exp(b) for b in blast]
    return qi, ki, kend, v.astype(bf16), dec, mask


def _gla_intra(qi, ki, vb, mask, h):
    ks = slice(h * GLA_DK, (h + 1) * GLA_DK)
    vs = slice(h * GLA_DV, (h + 1) * GLA_DV)
    att = lax.dot_general(qi[:, ks], ki[:, ks], (((1,), (1,)), ((), ())), preferred_element_type=f32)
    att = jnp.where(mask, att, 0.0).astype(bf16)
    return _dot(att, vb[:, vs])


def _gla_inter(qi, kend, vb, dec, S, h, reset):
    c = GLA_CHUNK
    ks = slice(h * GLA_DK, (h + 1) * GLA_DK)
    vs = slice(h * GLA_DV, (h + 1) * GLA_DV)
    nch = qi.shape[0] // c
    upd, dcol = [], []
    for n in range(nch):
        rs = slice(n * c, (n + 1) * c)
        upd.append(lax.dot_general(kend[rs, ks], vb[rs, vs], (((0,), (0,)), ((), ())), preferred_element_type=f32))
        d = jnp.transpose(jnp.broadcast_to(dec[n][:, ks], (GLA_DK, GLA_DK)))
        dcol.append(jnp.concatenate([d, d], axis=1))
    sh = jnp.where(reset, 0.0, S[h])
    outs = []
    for n in range(nch):
        rs = slice(n * c, (n + 1) * c)
        outs.append(_dot(qi[rs, ks], sh.astype(bf16)))
        sh = sh * dcol[n] + upd[n]
    S[h] = sh
    return jnp.concatenate(outs, axis=0)


def _interleave(order, **gens):
    def step(k):
        if k in gens:
            try:
                next(gens[k])
            except StopIteration:
                del gens[k]
    for k in order:
        step(k)
    while gens:
        for k in list(gens):
            step(k)


def _mix_prompt_kernel(xp_ref, xa_ref, xs_ref, sh0_ref, sc0_ref, vec_ref, wmain_ref, wlr_ref, wlr2_ref, rgbd_ref,
                       wbr_ref, wbg_ref, wo_ref, y_ref, h_ref, conv_ref, s_ref,
                       shn_ref, scn_ref, sornn_ref, sqkvg_ref, sglog_ref, sgates_ref,
                       u_s, xrb, yr_s, ornn_s, hc, tail, S, hbuf, pre_s, *, bpb):
    j = pl.program_id(0)
    rb = GLA_BLOCK
    sw = 2 * RNN_BW
    nw = 512

    @pl.when(j == 0)
    def _():
        u_s[...] = jnp.zeros_like(u_s)
        xrb[...] = jnp.zeros_like(xrb)
        yr_s[...] = jnp.zeros_like(yr_s)
        ornn_s[...] = jnp.zeros_like(ornn_s)
        hc[...] = jnp.zeros_like(hc)
        tail[...] = jnp.zeros_like(tail)
        S[...] = jnp.zeros_like(S)
        _sample_pre_kernel(xs_ref, sh0_ref, sc0_ref, vec_ref, wmain_ref, wlr_ref, wlr2_ref, rgbd_ref,
                           shn_ref, scn_ref, sornn_ref, sqkvg_ref, sglog_ref, sgates_ref)

    p3 = j % 3
    p2 = j % 2
    r2 = (j + 1) % 2
    a3 = (j + 1) % 3
    r_first = (j + bpb - 1) % bpb == 0
    a_first = (j + 2 * bpb - 2) % bpb == 0

    def project():
        u = _rms(xp_ref[...], vec_ref[R_NORM + 2:R_NORM + 3, :]).astype(bf16)
        u_s[p3] = u
        yield
        for c0 in range(0, D_RNN, sw):
            xrb[p2, SUBLANES:SUBLANES + rb, c0:c0 + sw] = _dot(u, _w(wmain_ref[:, c0:c0 + sw]))
            yield
        for c0 in range(0, D_RNN, nw):
            yr_s[p2, :, c0:c0 + nw] = _dot(u, _w(wmain_ref[:, D_RNN + c0:D_RNN + c0 + nw]))
            yield

    def recur():
        xrb[r2, 0:SUBLANES, :] = jnp.where(r_first, 0.0, tail[...])
        rc = ROW_CHUNK
        sub = lax.broadcasted_iota(jnp.int32, (SUBLANES, sw), 0)
        scan_steps = [(sh, sub >= sh) for sh in (1, 2, 4)]
        first_row = jnp.logical_and(sub == 0, r_first)
        for s in range(D_RNN // sw):
            cs = slice(s * sw, (s + 1) * sw)
            for r0 in range(0, rb, rc):
                ext = xrb[r2, r0:r0 + SUBLANES + rc, cs]
                xc = vec_ref[R_CONVB:R_CONVB + 1, cs]
                for t in range(CONV_W):
                    lag = CONV_W - 1 - t
                    tap = ext if lag == 0 else pltpu.roll(ext, lag, 0)
                    xc = xc + tap[SUBLANES:, :] * vec_ref[R_CONVW + t:R_CONVW + t + 1, cs]
                hbuf[r0:r0 + rc, cs] = xc
            yield
            pre_s[s] = _dot(hbuf[:, cs].astype(bf16), _w(rgbd_ref[s]))
            yield
            hprev = jnp.where(r_first, 0.0, hc[:, cs])
            for r0 in range(0, rb, rc):
                hs = []
                for g0 in range(r0, r0 + rc, SUBLANES):
                    gs = slice(g0, g0 + SUBLANES)
                    xc = hbuf[gs, cs]
                    a, mult, gi = _rglru_coeffs(pre_s[s, gs, 0:sw], pre_s[s, gs, sw:2 * sw], vec_ref, cs)
                    if g0 == 0:
                        mult = jnp.where(first_row, 1.0, mult)
                    b = mult * gi * xc
                    for sh, m in scan_steps:
                        b = jnp.where(m, a * pltpu.roll(b, sh, 0) + b, b)
                        a = jnp.where(m, a * pltpu.roll(a, sh, 0), a)
                    hg = a * hprev + b
                    hs.append(hg)
                    hprev = jnp.broadcast_to(hg[SUBLANES - 1:SUBLANES, :], (SUBLANES, sw))
                h = jnp.concatenate(hs, axis=0)
                ornn_s[r2, r0:r0 + rc, cs] = (h * _gelu_tanh(yr_s[r2, r0:r0 + rc, cs])).astype(bf16)
                if (r0 // rc) % 2 == 1:
                    yield
            hc[:, cs] = hprev
        tail[...] = xrb[r2, rb:rb + SUBLANES, :]

    def attend():
        u = u_s[a3]
        o_rnn = ornn_s[p2]
        proj = []
        for c0 in range(OFF_Q, OFF_OG, nw):
            proj.append(_dot(u, _w(wmain_ref[:, c0:c0 + nw])))
            yield
        proj = jnp.concatenate(proj, axis=1)
        glog = _decay_log(u, wlr_ref, wlr2_ref, vec_ref)
        yield
        qi, ki, kend, vb, dec, mask = _gla_prepare(proj[:, 0:D_QK], proj[:, D_QK:2 * D_QK], proj[:, 2 * D_QK:], glog)
        yield
        og = []
        for c0 in range(OFF_OG, OFF_LR, nw):
            og.append(_dot(u, _w(wmain_ref[:, c0:c0 + nw])))
            yield
        og = jnp.concatenate(og, axis=1)
        gate_by_col = {}
        o_heads = []
        gate_cols = list(range(0, 2 * D_MODEL, nw))
        for h in range(GLA_HEADS):
            hs = slice(h * GLA_DV, (h + 1) * GLA_DV)
            oh = _gla_intra(qi, ki, vb, mask, h) + _gla_inter(qi, kend, vb, dec, S, h, a_first)
            o_heads.append((_head_norm(oh, vec_ref, h) * _silu(og[:, hs])).astype(bf16))
            yield
            for c0 in gate_cols[h::GLA_HEADS]:
                gate_by_col[c0] = _dot(u, _w(wmain_ref[:, OFF_LR + c0:OFF_LR + c0 + nw]))
                yield
        gates = [gate_by_col[c0] for c0 in gate_cols]
        part_a = []
        for i, c0 in enumerate(range(0, D_MODEL, nw)):
            part_a.append(_sigmoid(gates[i]) * _dot(o_rnn, _w(wbr_ref[:, c0:c0 + nw])))
            yield
        o_gla = jnp.concatenate(o_heads, axis=1)
        merged = []
        ng = D_MODEL // nw
        for i, c0 in enumerate(range(0, D_MODEL, nw)):
            yb = _dot(o_gla, _w(wbg_ref[:, c0:c0 + nw]))
            merged.append((part_a[i] + _sigmoid(gates[ng + i]) * yb).astype(bf16))
            yield
        merged = jnp.concatenate(merged, axis=1)
        mix = []
        for c0 in range(0, D_MODEL, nw):
            mix.append(_dot(merged, _w(wo_ref[:, c0:c0 + nw])))
            yield
        mix = jnp.concatenate(mix, axis=1)
        y_ref[...] = xa_ref[...] + _rms(mix, vec_ref[R_NORM + 3:R_NORM + 4, :])
        yield

    _interleave("", A=attend(), R=recur(), P=project())

    @pl.when((j + bpb - 1) % bpb == bpb - 1)
    def _():
        h_ref[...] = hc[0:1, :]
        conv_ref[...] = tail[SUBLANES - (CONV_W - 1):SUBLANES, :]

    @pl.when((j + 2 * bpb - 2) % bpb == bpb - 1)
    def _():
        s_ref[...] = S[...]


def _mix_prompt_call(x2d, xs, sh0, sc0, vecs, wmain, wlr, wlr2, rgbd, wbr, wbg, wo, *, bsz, seq):
    rb = GLA_BLOCK
    bpb = seq // rb
    nblk = bsz * bpb
    ns = xs.shape[0]
    consts = (xs, sh0, sc0, vecs, wmain, wlr, wlr2, rgbd, wbr, wbg, wo)
    sample_outs = [((ns, D_RNN), f32), ((CONV_W - 1, ns, D_RNN), f32), ((ns, D_RNN), bf16),
                   ((ns, OFF_LR - OFF_Q), f32), ((ns, D_QK), f32), ((ns, 2 * D_MODEL), f32)]

    def blk(d):
        return lambda j: (jnp.clip(j - d, 0, nblk - 1), 0)

    def seq_of(d, nd):
        return lambda j: (jnp.clip(j - d, 0, nblk - 1) // bpb,) + (0,) * nd

    return pl.pallas_call(
        functools.partial(_mix_prompt_kernel, bpb=bpb),
        grid=(nblk + 2,),
        in_specs=[pl.BlockSpec((rb, D_MODEL), blk(0)), pl.BlockSpec((rb, D_MODEL), blk(2))]
        + [_const_spec(w.shape) for w in consts],
        out_specs=[pl.BlockSpec((rb, D_MODEL), blk(2)),
                   pl.BlockSpec((None, 1, D_RNN), seq_of(1, 2)),
                   pl.BlockSpec((None, CONV_W - 1, D_RNN), seq_of(1, 2)),
                   pl.BlockSpec((None, GLA_HEADS, GLA_DK, GLA_DV), seq_of(2, 3))]
        + [_const_spec(s) for s, _ in sample_outs],
        out_shape=[jax.ShapeDtypeStruct((nblk * rb, D_MODEL), f32),
                   jax.ShapeDtypeStruct((bsz, 1, D_RNN), f32),
                   jax.ShapeDtypeStruct((bsz, CONV_W - 1, D_RNN), f32),
                   jax.ShapeDtypeStruct((bsz, GLA_HEADS, GLA_DK, GLA_DV), f32)]
        + [jax.ShapeDtypeStruct(s, d) for s, d in sample_outs],
        scratch_shapes=[pltpu.VMEM((3, rb, D_MODEL), bf16),
                        pltpu.VMEM((2, rb + SUBLANES, D_RNN), f32),
                        pltpu.VMEM((2, rb, D_RNN), f32),
                        pltpu.VMEM((2, rb, D_RNN), bf16),
                        pltpu.VMEM((SUBLANES, D_RNN), f32),
                        pltpu.VMEM((SUBLANES, D_RNN), f32),
                        pltpu.VMEM((GLA_HEADS, GLA_DK, GLA_DV), f32),
                        pltpu.VMEM((rb, D_RNN), f32),
                        pltpu.VMEM((D_RNN // (2 * RNN_BW), rb, 4 * RNN_BW), f32)],
        compiler_params=pltpu.CompilerParams(dimension_semantics=("arbitrary",), vmem_limit_bytes=VMEM_LIMIT),
        name="mix_prompt",
    )(x2d, x2d, *consts)


def _sample_pre_kernel(x1_ref, h0_ref, c0_ref, vec_ref, wmain_ref, wlr_ref, wlr2_ref, rgbd_ref,
                       hn_ref, cn_ref, ornn_ref, qkvg_ref, glog_ref, gates_ref):
    u = _rms(x1_ref[...], vec_ref[R_NORM + 2:R_NORM + 3, :]).astype(bf16)
    xy = _dot(u, _w(wmain_ref[:, 0:2 * D_RNN]))
    xr = xy[:, :D_RNN]
    yr = xy[:, D_RNN:]
    xc = vec_ref[R_CONVB:R_CONVB + 1, :]
    for j in range(CONV_W - 1):
        xc = xc + c0_ref[j] * vec_ref[R_CONVW + j:R_CONVW + j + 1, :]
    xc = xc + xr * vec_ref[R_CONVW + CONV_W - 1:R_CONVW + CONV_W, :]
    for j in range(CONV_W - 2):
        cn_ref[j] = c0_ref[j + 1]
    cn_ref[CONV_W - 2] = xr
    ra, ri = _rglru_preact(xc, rgbd_ref)
    a, mult, gi = _rglru_coeffs(ra, ri, vec_ref)
    h = a * h0_ref[...] + mult * gi * xc
    hn_ref[...] = h
    ornn_ref[...] = (h * _gelu_tanh(yr)).astype(bf16)
    qkvg_ref[...] = _dot(u, _w(wmain_ref[:, OFF_Q:OFF_LR]))
    glog_ref[...] = _decay_log(u, wlr_ref, wlr2_ref, vec_ref)
    gates_ref[...] = _dot(u, _w(wmain_ref[:, OFF_LR:OFF_LR + 2 * D_MODEL]))


def _col_bcast(rows8, width):
    tiled = jnp.concatenate([rows8] * (LANES // SUBLANES), axis=0)
    tt = jnp.transpose(tiled)
    return [jnp.broadcast_to(tt[:, j:j + 1], (LANES, width)) for j in range(SUBLANES)]


def _sample_gla_update(q_ref, k_ref, v_ref, g_ref, s0_ref, sn_ref, o_ref):
    scale = GLA_DK ** -0.5
    ns = q_ref.shape[0]
    pad8 = lambda a: jnp.concatenate([a] * (SUBLANES // ns), axis=0)
    for h in range(GLA_HEADS):
        ks = slice(h * GLA_DK, (h + 1) * GLA_DK)
        vs = slice(h * GLA_DV, (h + 1) * GLA_DV)
        dcols = _col_bcast(pad8(jnp.exp(g_ref[:, ks])), GLA_DV)
        kcols = _col_bcast(pad8(k_ref[:, ks]), GLA_DV)
        q8 = pad8(q_ref[:, ks] * scale).astype(bf16)
        for j in range(ns):
            vrow = v_ref[j:j + 1, vs]
            sn = dcols[j] * s0_ref[j, h] + kcols[j] * vrow
            sn_ref[j, h] = sn
            o_ref[j:j + 1, vs] = _dot(q8, sn.astype(bf16))[j:j + 1, :]


def _ffn_gla_kernel(x_ref, vec_ref, wgu_ref, wd_ref, q_ref, k_ref, v_ref, g_ref, s0_ref, y_ref, sn_ref, o_ref, *,
                    row_pre):
    y_ref[...] = _ffn(x_ref[...], vec_ref[row_pre:row_pre + 1, :], vec_ref[row_pre + 1:row_pre + 2, :],
                      wgu_ref, wd_ref)
    _sample_gla_update(q_ref, k_ref, v_ref, g_ref, s0_ref, sn_ref, o_ref)


def _ffn_gla_call(x, vecs, wgu, wd, q, k, v, g, s0, *, row_pre, tm, name):
    n = x.shape[0]
    nsteps = n // tm
    nseq = q.shape[0]
    per = nseq // nsteps
    assert per * nsteps == nseq and SUBLANES % per == 0
    grp = lambda a: a.reshape(nsteps, per, a.shape[-1])
    row3 = lambda w: pl.BlockSpec((None, per, w), lambda i: (i, 0, 0))
    sb = pl.BlockSpec((per, GLA_HEADS, GLA_DK, GLA_DV), lambda i: (i, 0, 0, 0))
    y, sn, o = pl.pallas_call(
        functools.partial(_ffn_gla_kernel, row_pre=row_pre),
        grid=(nsteps,),
        in_specs=[pl.BlockSpec((tm, D_MODEL), lambda i: (i, 0)),
                  _const_spec(vecs.shape), _const_spec(wgu.shape), _const_spec(wd.shape),
                  row3(D_QK), row3(D_QK), row3(D_V), row3(D_QK), sb],
        out_specs=[pl.BlockSpec((tm, D_MODEL), lambda i: (i, 0)), sb, row3(D_V)],
        out_shape=[jax.ShapeDtypeStruct((n, D_MODEL), f32), jax.ShapeDtypeStruct(s0.shape, f32),
                   jax.ShapeDtypeStruct((nsteps, per, D_V), f32)],
        compiler_params=pltpu.CompilerParams(dimension_semantics=("arbitrary",), vmem_limit_bytes=VMEM_LIMIT),
        name=name,
    )(x, vecs, wgu, wd, grp(q), grp(k), grp(v), grp(g), s0)
    return y, sn, o.reshape(nseq, D_V)


def _sample_post_kernel(x1_ref, ornn_ref, o_ref, og_ref, gates_ref, vec_ref, wbr_ref, wbg_ref, wo_ref, wgu_ref, wd_ref,
                        y_ref):
    o = o_ref[...]
    og = og_ref[...]
    o_gla = jnp.concatenate(
        [_head_norm(o[:, h * GLA_DV:(h + 1) * GLA_DV], vec_ref, h) * _silu(og[:, h * GLA_DV:(h + 1) * GLA_DV])
         for h in range(GLA_HEADS)], axis=1).astype(bf16)
    gates = gates_ref[...]
    ya = _dot(ornn_ref[...], _w(wbr_ref[...]))
    yb = _dot(o_gla, _w(wbg_ref[...]))
    merged = _sigmoid(gates[:, :D_MODEL]) * ya + _sigmoid(gates[:, D_MODEL:]) * yb
    mix = _dot(merged.astype(bf16), _w(wo_ref[...]))
    x2 = x1_ref[...] + _rms(mix, vec_ref[R_NORM + 3:R_NORM + 4, :])
    y_ref[...] = _ffn(x2, vec_ref[R_NORM + 4:R_NORM + 5, :], vec_ref[R_NORM + 5:R_NORM + 6, :], wgu_ref, wd_ref)


def _sample_post_call(x1, ornn, o, og, gates, vecs, wbr, wbg, wo, wgu, wd):
    n = x1.shape[0]
    args = (x1, ornn, o, og, gates, vecs, wbr, wbg, wo, wgu, wd)
    return pl.pallas_call(
        _sample_post_kernel,
        grid=(1,),
        in_specs=[_const_spec(a.shape) for a in args],
        out_specs=pl.BlockSpec((n, D_MODEL), lambda i: (0, 0)),
        out_shape=jax.ShapeDtypeStruct((n, D_MODEL), f32),
        compiler_params=pltpu.CompilerParams(dimension_semantics=("arbitrary",), vmem_limit_bytes=VMEM_LIMIT),
        name="sample_post",
    )(*args)


def _pack_kernel(w_ref, o_ref):
    o_ref[...] = pltpu.bitcast(w_ref[...].astype(bf16), jnp.uint32)


def _pack(w, *, bn=None, name):
    nl, k, n = w.shape
    if bn is None:
        bn = PACK_COLS if n % PACK_COLS == 0 else n
    assert n % bn == 0
    return pl.pallas_call(
        _pack_kernel,
        grid=(nl, n // bn),
        in_specs=[pl.BlockSpec((None, k, bn), lambda l, j: (l, 0, j))],
        out_specs=pl.BlockSpec((None, k // 2, bn), lambda l, j: (l, 0, j)),
        out_shape=jax.ShapeDtypeStruct((nl, k // 2, n), jnp.uint32),
        compiler_params=pltpu.CompilerParams(dimension_semantics=("arbitrary", "arbitrary"),
                                             vmem_limit_bytes=VMEM_LIMIT),
        name=name,
    )(w)


def _pack_t_kernel(wt_ref, o_ref, *, keep):
    w = jnp.transpose(wt_ref[...])
    if keep < w.shape[1]:
        w = jnp.where(lax.broadcasted_iota(jnp.int32, w.shape, 1) < keep, w, 0.0)
    o_ref[...] = pltpu.bitcast(w.astype(bf16), jnp.uint32)


def _pack_t(wt, *, row0, nrows, keep=None, name):
    _, _, k = wt.shape
    bn = min(PACK_COLS, nrows)
    assert nrows % bn == 0 and row0 % SUBLANES == 0
    return pl.pallas_call(
        functools.partial(_pack_t_kernel, keep=nrows if keep is None else keep),
        grid=(nrows // bn,),
        in_specs=[pl.BlockSpec((None, pl.Element(bn), pl.Element(k)),
                               lambda j: (0, pl.multiple_of(row0 + j * bn, SUBLANES), 0))],
        out_specs=pl.BlockSpec((k // 2, bn), lambda j: (0, j)),
        out_shape=jax.ShapeDtypeStruct((k // 2, nrows), jnp.uint32),
        compiler_params=pltpu.CompilerParams(dimension_semantics=("arbitrary",), vmem_limit_bytes=VMEM_LIMIT),
        name=name,
    )(wt)


def _block_diag_gates(w_a, w_x):
    def bd(w):
        w = w.reshape(RNN_BLOCKS // 2, 2, RNN_BW, RNN_BW)
        z = jnp.zeros_like(w[:, 0])
        top = jnp.concatenate([w[:, 0], z], axis=2)
        bot = jnp.concatenate([z, w[:, 1]], axis=2)
        return jnp.concatenate([top, bot], axis=1)
    return jnp.concatenate([bd(w_a), bd(w_x)], axis=2)


def kernel(x_prompt, x_sample, state_rnn_h, state_rnn_conv, state_gla, norm_gains, ffn1_w_gu, ffn1_w_down, w_in, conv_w, conv_b, rg_w_a, rg_b_a, rg_w_x, rg_b_x, rg_lambda, gla_w_lr, gla_b_lr, gla_norm_g, w_branch_rnn, w_branch_gla, w_out, ffn2_w_gu, ffn2_w_down):
    assert w_in.shape == (1, D_MODEL, D_IN) and x_sample.shape[1] == 1
    bsz, seq, _ = x_prompt.shape
    nsmp = x_sample.shape[0]

    vecs = jnp.concatenate([
        norm_gains[0], conv_w[0], conv_b, rg_b_a, rg_b_x, rg_lambda,
        jnp.pad(gla_b_lr, ((0, 0), (0, D_MODEL - D_QK))), jnp.tile(gla_norm_g, (1, GLA_HEADS))], axis=0).astype(f32)
    w1gu = _pack(ffn1_w_gu, bn=D_FF // 2, name="pack_w1gu")[0]
    w1d = _pack(ffn1_w_down, name="pack_w1d")[0]
    w_in_t = jnp.swapaxes(w_in, 1, 2)
    wlr = _pack_t(w_in_t, row0=OFF_LR, nrows=LANES, keep=GLA_RANK, name="pack_wlr")
    wlr2 = _pack(jnp.pad(gla_w_lr, ((0, 0), (0, LANES - GLA_RANK), (0, 0))), name="pack_wlr2")[0]
    rgbd = _pack(_block_diag_gates(rg_w_a[0], rg_w_x[0]), name="pack_rgbd")

    xp = x_prompt.reshape(bsz * seq, D_MODEL)
    xs = x_sample.reshape(nsmp, D_MODEL)
    x1, x1s, w2gu, w2d, wmain, wbr, wbg, wo = _ffn_call(
        xp, xs, vecs, w1gu, w1d, row_pre=R_NORM, tm=512, name="ffn1",
        side=(("cols", ffn2_w_gu, 2 * LANES), ("rows", ffn2_w_down, LANES), ("t", w_in_t, 2 * LANES),
              ("cols", w_branch_rnn, 2 * LANES), ("cols", w_branch_gla, 2 * LANES), ("cols", w_out, 2 * LANES)))
    c0 = jnp.swapaxes(state_rnn_conv[0], 0, 1)
    x2, hp, cp, sp, hs, cs, ornn, qkvg, glog, gates = _mix_prompt_call(
        x1, x1s, state_rnn_h[0], c0, vecs, wmain, wlr, wlr2, rgbd, wbr, wbg, wo, bsz=bsz, seq=seq)

    yp, ss, osmp = _ffn_gla_call(x2, vecs, w2gu, w2d, qkvg[:, 0:D_QK], qkvg[:, D_QK:2 * D_QK],
                                 qkvg[:, 2 * D_QK:2 * D_QK + D_V], glog, state_gla[0],
                                 row_pre=R_NORM + 4, tm=512, name="ffn2_gla")
    ys = _sample_post_call(x1s, ornn, osmp, qkvg[:, 2 * D_QK + D_V:], gates, vecs, wbr, wbg, wo, w2gu, w2d)

    return (yp.reshape(bsz, seq, D_MODEL), ys.reshape(nsmp, 1, D_MODEL),
            hp.reshape(1, bsz, D_RNN), cp[None], sp[None],
            hs[None], jnp.swapaxes(cs, 0, 1)[None], ss[None])
```
